```python
import math
import jax, jax.numpy as jnp
from jax import lax
import numpy as np

D_MODEL = 2048
BATCH = 8
SEQ = 2048
DEPTH = 2

CHUNK = 64
EPS = 1e-6
RET_HEADS = 8
RET_QK_DIM = D_MODEL // RET_HEADS
RET_V_DIM = 2 * RET_QK_DIM
RET_HV = RET_HEADS * RET_V_DIM
ROPE_BASE = 10000.0
ATT_HEADS = 16
ATT_HEAD_DIM = D_MODEL // ATT_HEADS
LEFT_CHUNKS = 8
BAND = (LEFT_CHUNKS + 1) * CHUNK
REL_CLIP = 128
NEG_INF = -1e30
D_FF = 5632
N_EXPERTS = 8
TOP_K = 2
D_FF_EXPERT = 7168
N_A = (DEPTH + 1) // 2
N_B = DEPTH // 2

kernel_name = "hybrid_retention_chunkattn_moe_trunk"


def rmsnorm(x, g):
    xf = x.astype(jnp.float32)
    y = xf * lax.rsqrt(jnp.mean(xf * xf, axis=-1, keepdims=True) + EPS)
    return (y * g.astype(jnp.float32)).astype(x.dtype)


def rotary(x, pos):
    dh = x.shape[-1]
    inv_freq = ROPE_BASE ** (-jnp.arange(0, dh, 2, dtype=jnp.float32) / dh)
    ang = pos[:, None] * inv_freq[None, :]
    cos = jnp.cos(ang)[None, :, None, :]
    sin = jnp.sin(ang)[None, :, None, :]
    xf = x.astype(jnp.float32)
    x1, x2 = jnp.split(xf, 2, axis=-1)
    out = jnp.concatenate([x1 * cos - x2 * sin, x1 * sin + x2 * cos], axis=-1)
    return out.astype(x.dtype)


def retention(h, w_in, w_out):
    B, S, _ = h.shape
    NC = S // CHUNK
    proj = h @ w_in
    q, k, v, g = jnp.split(proj, [D_MODEL, 2 * D_MODEL, 2 * D_MODEL + RET_HV], axis=-1)
    q = q.reshape(B, S, RET_HEADS, RET_QK_DIM)
    k = k.reshape(B, S, RET_HEADS, RET_QK_DIM)
    v = v.reshape(B, S, RET_HEADS, RET_V_DIM)
    pos = jnp.arange(S, dtype=jnp.float32)
    q = rotary(q, pos)
    k = rotary(k, pos) * (RET_QK_DIM ** -0.5)

    gamma = 1.0 - jnp.exp2(-5.0 - jnp.arange(RET_HEADS, dtype=jnp.float32))
    log_g = jnp.log(gamma)
    r = jnp.arange(CHUNK, dtype=jnp.float32)
    intra_decay = jnp.exp(log_g[:, None, None] * jnp.abs(r[:, None] - r[None, :]))
    q_decay = jnp.exp(log_g[:, None] * (r + 1.0))
    k_decay = jnp.exp(log_g[:, None] * (CHUNK - 1.0 - r))
    chunk_decay = jnp.exp(log_g * CHUNK)

    def to_chunks(t):
        return t.reshape(B, NC, CHUNK, t.shape[2], t.shape[3]).transpose(1, 0, 3, 2, 4)

    qc, kc, vc = to_chunks(q), to_chunks(k), to_chunks(v)

    def step(state, inp):
        qi, ki, vi = inp
        s = jnp.einsum('bhnd,bhmd->bhnm', qi, ki).astype(jnp.float32) * intra_decay[None]
        inner = jnp.einsum('bhnm,bhmv->bhnv', s, vi.astype(jnp.float32))
        cross = jnp.einsum('bhnd,bhdv->bhnv', qi.astype(jnp.float32), state) * q_decay[None, :, :, None]
        kd = ki.astype(jnp.float32) * k_decay[None, :, :, None]
        new_state = state * chunk_decay[None, :, None, None] + jnp.einsum(
            'bhmd,bhmv->bhdv', kd, vi.astype(jnp.float32))
        return new_state, inner + cross

    state0 = jnp.zeros((B, RET_HEADS, RET_QK_DIM, RET_V_DIM), jnp.float32)
    _, o = lax.scan(step, state0, (qc, kc, vc))
    o = o.transpose(1, 0, 3, 2, 4).reshape(B, S, RET_HEADS, RET_V_DIM)
    o = o * lax.rsqrt(jnp.mean(o * o, axis=-1, keepdims=True) + EPS)
    o = o.reshape(B, S, RET_HV)
    y = (jax.nn.silu(g.astype(jnp.float32)) * o).astype(h.dtype)
    return y @ w_out


def chunk_attention(h, w_in, w_out, rel_bias):
    B, S, _ = h.shape
    NC = S // CHUNK
    pad = LEFT_CHUNKS * CHUNK
    proj = h @ w_in
    q, k, v = jnp.split(proj, 3, axis=-1)
    q = q.reshape(B, S, ATT_HEADS, ATT_HEAD_DIM)
    k = k.reshape(B, S, ATT_HEADS, ATT_HEAD_DIM)
    v = v.reshape(B, S, ATT_HEADS, ATT_HEAD_DIM)
    kp = jnp.pad(k, ((0, 0), (pad, 0), (0, 0), (0, 0)))
    vp = jnp.pad(v, ((0, 0), (pad, 0), (0, 0), (0, 0)))

    s_idx = jnp.arange(BAND)
    r_idx = jnp.arange(CHUNK)
    dist = r_idx[:, None] + pad - s_idx[None, :]
    bias = rel_bias.astype(jnp.float32)[:, jnp.clip(dist, -REL_CLIP, REL_CLIP) + REL_CLIP]
    scale = ATT_HEAD_DIM ** -0.5

    def one_chunk(c):
        qi = lax.dynamic_slice_in_dim(q, c * CHUNK, CHUNK, axis=1)
        ki = lax.dynamic_slice_in_dim(kp, c * CHUNK, BAND, axis=1)
        vi = lax.dynamic_slice_in_dim(vp, c * CHUNK, BAND, axis=1)
        valid = (c * CHUNK - pad + s_idx) >= 0
        logits = jnp.einsum('bqhd,bkhd->bhqk', qi, ki).astype(jnp.float32) * scale + bias[None]
        logits = jnp.where(valid[None, None, None, :], logits, jnp.float32(NEG_INF))
        p = jax.nn.softmax(logits, axis=-1)
        return jnp.einsum('bhqk,bkhd->bqhd', p.astype(vi.dtype), vi)

    out = lax.map(one_chunk, jnp.arange(NC))
    out = out.transpose(1, 0, 2, 3, 4).reshape(B, S, D_MODEL)
    return out @ w_out


def swiglu(h, w_in, w_out):
    gu = h @ w_in
    g, u = jnp.split(gu, 2, axis=-1)
    return (jax.nn.silu(g.astype(jnp.float32)) * u.astype(jnp.float32)).astype(h.dtype) @ w_out


def moe_swiglu(h, w_router, w_in, w_out):
    B, S, D = h.shape
    ht = h.reshape(B * S, D)
    logits = (ht @ w_router).astype(jnp.float32)
    top_val, top_idx = lax.top_k(logits, TOP_K)
    gates = jax.nn.softmax(top_val, axis=-1)
    combine = jnp.sum(jax.nn.one_hot(top_idx, N_EXPERTS, dtype=jnp.float32) * gates[..., None], axis=1)
    out = jnp.zeros((B * S, D), jnp.float32)
    for e in range(N_EXPERTS):
        y = swiglu(ht, w_in[e], w_out[e]).astype(jnp.float32)
        out = out + combine[:, e:e + 1] * y
    return out.astype(h.dtype).reshape(B, S, D)


def setup_inputs(seed: int = 0) -> dict:
    key = jax.random.key(seed)
    ks = jax.random.split(key, 16)
    f32 = jnp.float32

    def w(k, shape, fan_in):
        return jax.random.normal(k, shape, f32) * (fan_in ** -0.5)

    return {
        "x": jax.random.normal(ks[0], (BATCH, SEQ, D_MODEL), f32),
        "ln_mix": 1.0 + 0.02 * jax.random.normal(ks[1], (DEPTH, D_MODEL), f32),
        "ln_ffn": 1.0 + 0.02 * jax.random.normal(ks[2], (DEPTH, D_MODEL), f32),
        "ln_final": 1.0 + 0.02 * jax.random.normal(ks[3], (D_MODEL,), f32),
        "ret_w_in": w(ks[4], (N_A, D_MODEL, 2 * D_MODEL + 2 * RET_HV), D_MODEL),
        "ret_w_out": w(ks[5], (N_A, RET_HV, D_MODEL), RET_HV),
        "ffn_w_in": w(ks[6], (N_A, D_MODEL, 2 * D_FF), D_MODEL),
        "ffn_w_out": w(ks[7], (N_A, D_FF, D_MODEL), D_FF),
        "att_w_in": w(ks[8], (N_B, D_MODEL, 3 * D_MODEL), D_MODEL),
        "att_w_out": w(ks[9], (N_B, D_MODEL, D_MODEL), D_MODEL),
        "att_rel_bias": 0.2 * jax.random.normal(ks[10], (N_B, ATT_HEADS, 2 * REL_CLIP + 1), f32),
        "moe_router": w(ks[11], (N_B, D_MODEL, N_EXPERTS), D_MODEL),
        "moe_w_in": w(ks[12], (N_B, N_EXPERTS, D_MODEL, 2 * D_FF_EXPERT), D_MODEL),
        "moe_w_out": w(ks[13], (N_B, N_EXPERTS, D_FF_EXPERT, D_MODEL), D_FF_EXPERT),
    }


def reference(x, ln_mix, ln_ffn, ln_final, ret_w_in, ret_w_out, ffn_w_in, ffn_w_out,
              att_w_in, att_w_out, att_rel_bias, moe_router, moe_w_in, moe_w_out):
    for i in range(DEPTH):
        j = i // 2
        h = rmsnorm(x, ln_mix[i])
        if i % 2 == 0:
            x = x + retention(h, ret_w_in[j], ret_w_out[j])
        else:
            x = x + chunk_attention(h, att_w_in[j], att_w_out[j], att_rel_bias[j])
        h = rmsnorm(x, ln_ffn[i])
        if i % 2 == 0:
            x = x + swiglu(h, ffn_w_in[j], ffn_w_out[j])
        else:
            x = x + moe_swiglu(h, moe_router[j], moe_w_in[j], moe_w_out[j])
    return rmsnorm(x, ln_final)
```

```python
import functools

import jax
import jax.numpy as jnp
from jax import lax
from jax.experimental import pallas as pl
from jax.experimental.pallas import tpu as pltpu

D_MODEL = 2048
CHUNK = 64
EPS = 1e-6
RET_HEADS = 8
RET_QK_DIM = D_MODEL // RET_HEADS
RET_V_DIM = 2 * RET_QK_DIM
RET_HV = RET_HEADS * RET_V_DIM
ROPE_BASE = 10000.0
ATT_HEADS = 16
ATT_HEAD_DIM = D_MODEL // ATT_HEADS
LEFT_CHUNKS = 8
REL_CLIP = 128
NEG_INF = -1e30
N_EXPERTS = 8

LANES = 128
MIB = 1024 * 1024

F32 = jnp.float32
BF16 = jnp.bfloat16


def _params(semantics, vmem_mib):
    return pltpu.CompilerParams(
        dimension_semantics=semantics, vmem_limit_bytes=vmem_mib * MIB)


def _silu(x):
    return x * (1.0 / (1.0 + jnp.exp(-x)))


def _rmsnorm_body(x_ref, g_ref, o_ref):
    x = x_ref[...]
    ms = jnp.mean(x * x, axis=-1, keepdims=True)
    o_ref[...] = (x * lax.rsqrt(ms + EPS) * g_ref[...]).astype(o_ref.dtype)


def _rmsnorm(x2, g, out_dtype, tm=512):
    t, d = x2.shape
    return pl.pallas_call(
        _rmsnorm_body,
        grid=(t // tm,),
        in_specs=[pl.BlockSpec((tm, d), lambda i: (i, 0)),
                  pl.BlockSpec((1, d), lambda i: (0, 0))],
        out_specs=pl.BlockSpec((tm, d), lambda i: (i, 0)),
        out_shape=jax.ShapeDtypeStruct((t, d), out_dtype),
        compiler_params=_params(("arbitrary",), 32),
        name="rmsnorm",
    )(x2, g.reshape(1, d))


def _matmul_body(*refs, has_res):
    if has_res:
        x_ref, w_ref, r_ref, o_ref, wb_ref = refs
    else:
        x_ref, w_ref, o_ref, wb_ref = refs

    @pl.when(pl.program_id(1) == 0)
    def _():
        wb_ref[...] = w_ref[...].astype(BF16)

    acc = jnp.dot(x_ref[...], wb_ref[...], preferred_element_type=F32)
    if has_res:
        acc = acc + r_ref[...]
    o_ref[...] = acc.astype(o_ref.dtype)


def _matmul(x, w, res, out_dtype, tm, tn, name):
    m, k = x.shape
    n = w.shape[1]
    in_specs = [pl.BlockSpec((tm, k), lambda j, i: (i, 0)),
                pl.BlockSpec((k, tn), lambda j, i: (0, j))]
    args = [x, w]
    if res is not None:
        in_specs.append(pl.BlockSpec((tm, tn), lambda j, i: (i, j)))
        args.append(res)
    return pl.pallas_call(
        functools.partial(_matmul_body, has_res=res is not None),
        grid=(n // tn, m // tm),
        in_specs=in_specs,
        out_specs=pl.BlockSpec((tm, tn), lambda j, i: (i, j)),
        out_shape=jax.ShapeDtypeStruct((m, n), out_dtype),
        scratch_shapes=[pltpu.VMEM((k, tn), BF16)],
        compiler_params=_params(("arbitrary", "arbitrary"), 52),
        name=name,
    )(*args)


def _retention_body(q_ref, k_ref, v_ref, g_ref, cos_ref, sin_ref, o_ref, state_ref,
                    *, block_rows, seq):
    L = block_rows
    half = RET_QK_DIM // 2
    head = pl.program_id(1)

    def log_gamma(rows):
        headf = (jnp.zeros((rows, 1), jnp.int32) + head).astype(F32)
        return jnp.log(1.0 - jnp.exp2(-5.0 - headf))

    log_g = log_gamma(L)
    n_i = lax.broadcasted_iota(jnp.int32, (L, L), 0)
    m_i = lax.broadcasted_iota(jnp.int32, (L, L), 1)
    dist = jnp.abs(n_i - m_i).astype(F32)
    visible = (m_i >> 6) <= (n_i >> 6)
    decay = jnp.where(visible, jnp.exp(log_g * dist), 0.0)
    r = lax.broadcasted_iota(jnp.int32, (L, 1), 0).astype(F32)
    q_decay = jnp.exp(log_g * (r + 1.0))
    k_decay = jnp.exp(log_g * (L - 1.0 - r))
    block_decay = jnp.exp(log_gamma(RET_QK_DIM) * float(L))
    scale = RET_QK_DIM ** -0.5

    state_ref[...] = jnp.zeros_like(state_ref)

    def body(s, carry):
        r0 = pl.multiple_of(s * L, L)
        cos = cos_ref[pl.ds(r0, L), :]
        sin = sin_ref[pl.ds(r0, L), :]

        def rot(t):
            t1 = t[:, :half]
            t2 = t[:, half:]
            return jnp.concatenate([t1 * cos - t2 * sin, t1 * sin + t2 * cos], axis=1)

        q = rot(q_ref[pl.ds(r0, L), :].astype(F32)).astype(BF16)
        kf = rot(k_ref[pl.ds(r0, L), :].astype(F32)) * scale
        v = v_ref[pl.ds(r0, L), :]
        s_qk = lax.dot_general(q, kf.astype(BF16), (((1,), (1,)), ((), ())),
                               preferred_element_type=F32) * decay
        inner = jnp.dot(s_qk.astype(BF16), v, preferred_element_type=F32)
        st = state_ref[...]
        cross = jnp.dot(q, st.astype(BF16), preferred_element_type=F32) * q_decay
        kd_t = (kf * k_decay).T.astype(BF16)
        state_ref[...] = st * block_decay + jnp.dot(kd_t, v, preferred_element_type=F32)
        o = inner + cross
        o = o * lax.rsqrt(jnp.mean(o * o, axis=-1, keepdims=True) + EPS)
        gate = g_ref[pl.ds(r0, L), :].astype(F32)
        o_ref[pl.ds(r0, L), :] = (_silu(gate) * o).astype(o_ref.dtype)
        return carry

    lax.fori_loop(0, seq // L, body, 0)


def _retention(proj, cos, sin, block_rows=256):
    b, s, _ = proj.shape
    nq = D_MODEL // RET_QK_DIM
    nv = RET_HV // RET_V_DIM
    return pl.pallas_call(
        functools.partial(_retention_body, block_rows=block_rows, seq=s),
        grid=(b, RET_HEADS),
        in_specs=[
            pl.BlockSpec((None, s, RET_QK_DIM), lambda i, h: (i, 0, h)),
            pl.BlockSpec((None, s, RET_QK_DIM), lambda i, h: (i, 0, nq + h)),
            pl.BlockSpec((None, s, RET_V_DIM), lambda i, h: (i, 0, nv + h)),
            pl.BlockSpec((None, s, RET_V_DIM), lambda i, h: (i, 0, 2 * nv + h)),
            pl.BlockSpec((s, RET_QK_DIM // 2), lambda i, h: (0, 0)),
            pl.BlockSpec((s, RET_QK_DIM // 2), lambda i, h: (0, 0)),
        ],
        out_specs=pl.BlockSpec((None, s, RET_V_DIM), lambda i, h: (i, 0, h)),
        out_shape=jax.ShapeDtypeStruct((b, s, RET_HV), BF16),
        scratch_shapes=[pltpu.VMEM((RET_QK_DIM, RET_V_DIM), F32)],
        compiler_params=_params(("arbitrary", "arbitrary"), 40),
        name="retention_core",
    )(proj, proj, proj, proj, cos, sin)


ATT_PAD = LEFT_CHUNKS * CHUNK
ATT_LQ = 256
ATT_WIN = ATT_LQ + ATT_PAD
ATT_ROLL_W = ATT_LQ + ATT_WIN
REL_PAD = 384


def _attention_body(q_ref, k_ref, v_ref, rb_ref, o_ref, kp_ref, vp_ref, bm_ref, *, seq):
    @pl.when(pl.program_id(1) == 0)
    def _():
        t = lax.broadcasted_iota(jnp.int32, (REL_PAD, ATT_ROLL_W), 1)
        j = lax.broadcasted_iota(jnp.int32, (REL_PAD, ATT_ROLL_W), 0)
        idx = jnp.where(t <= ATT_WIN, jnp.clip(ATT_PAD - t, -REL_CLIP, REL_CLIP),
                        REL_CLIP) + REL_CLIP
        onehot = jnp.where(idx == j, 1.0, 0.0).astype(BF16)
        rb = rb_ref[...]
        hi = rb.astype(BF16)
        rest = rb - hi.astype(F32)
        mid = rest.astype(BF16)
        lo = (rest - mid.astype(F32)).astype(BF16)
        g = (jnp.dot(hi, onehot, preferred_element_type=F32)
             + jnp.dot(mid, onehot, preferred_element_type=F32)
             + jnp.dot(lo, onehot, preferred_element_type=F32))
        rows = jnp.broadcast_to(g[0:1, :], (ATT_LQ, ATT_ROLL_W))
        row = lax.broadcasted_iota(jnp.int32, (ATT_LQ, ATT_ROLL_W), 0)
        shift = 1
        while shift < ATT_LQ:
            rows = jnp.where((row & shift) != 0, pltpu.roll(rows, shift, 1), rows)
            shift *= 2
        qc = lax.broadcasted_iota(jnp.int32, (ATT_LQ, ATT_WIN), 0) >> 6
        kc = lax.broadcasted_iota(jnp.int32, (ATT_LQ, ATT_WIN), 1) >> 6
        ahead = kc - qc
        banded = jnp.where(ahead <= LEFT_CHUNKS, rows[:, :ATT_WIN], NEG_INF)
        bm_ref[...] = jnp.where(ahead >= 0, banded, NEG_INF)

    kp_ref[0:ATT_PAD, :] = jnp.zeros((ATT_PAD, ATT_HEAD_DIM), BF16)
    vp_ref[0:ATT_PAD, :] = jnp.zeros((ATT_PAD, ATT_HEAD_DIM), BF16)
    kp_ref[ATT_PAD:, :] = k_ref[...]
    vp_ref[ATT_PAD:, :] = v_ref[...]
    scale = ATT_HEAD_DIM ** -0.5

    def body(sb, carry):
        q0 = pl.multiple_of(sb * ATT_LQ, ATT_LQ)
        q = q_ref[pl.ds(q0, ATT_LQ), :]
        kw = kp_ref[pl.ds(q0, ATT_WIN), :]
        vw = vp_ref[pl.ds(q0, ATT_WIN), :]
        logits = lax.dot_general(q, kw, (((1,), (1,)), ((), ())),
                                 preferred_element_type=F32) * scale + bm_ref[...]
        kpos = lax.broadcasted_iota(jnp.int32, (ATT_LQ, ATT_WIN), 1) + (q0 - ATT_PAD)
        logits = jnp.where(kpos >= 0, logits, NEG_INF)
        mx = jnp.max(logits, axis=-1, keepdims=True)
        p = jnp.exp(logits - mx)
        denom = jnp.sum(p, axis=-1, keepdims=True)
        o = jnp.dot(p.astype(BF16), vw, preferred_element_type=F32) / denom
        o_ref[pl.ds(q0, ATT_LQ), :] = o.astype(o_ref.dtype)
        return carry

    lax.fori_loop(0, seq // ATT_LQ, body, 0)


def _attention(qkv, rel_bias):
    b, s, _ = qkv.shape
    rb = jnp.zeros((ATT_HEADS, 8, REL_PAD), F32).at[:, 0, :2 * REL_CLIP + 1].set(rel_bias)
    return pl.pallas_call(
        functools.partial(_attention_body, seq=s),
        grid=(ATT_HEADS, b),
        in_specs=[
            pl.BlockSpec((None, s, ATT_HEAD_DIM), lambda h, i: (i, 0, h)),
            pl.BlockSpec((None, s, ATT_HEAD_DIM), lambda h, i: (i, 0, ATT_HEADS + h)),
            pl.BlockSpec((None, s, ATT_HEAD_DIM), lambda h, i: (i, 0, 2 * ATT_HEADS + h)),
            pl.BlockSpec((None, 8, REL_PAD), lambda h, i: (h, 0, 0)),
        ],
        out_specs=pl.BlockSpec((None, s, ATT_HEAD_DIM), lambda h, i: (i, 0, h)),
        out_shape=jax.ShapeDtypeStruct((b, s, D_MODEL), BF16),
        scratch_shapes=[pltpu.VMEM((s + ATT_PAD, ATT_HEAD_DIM), BF16),
                        pltpu.VMEM((s + ATT_PAD, ATT_HEAD_DIM), BF16),
                        pltpu.VMEM((ATT_LQ, ATT_WIN), F32)],
        compiler_params=_params(("arbitrary", "arbitrary"), 32),
        name="chunk_attention",
    )(qkv, qkv, qkv, rb)


def _ffn_body(te_ref, nu_ref, *refs, nj, has_res):
    del te_ref
    if has_res:
        x_ref, wg_ref, wu_ref, wo_ref, r_ref, o_ref, acc_ref = refs
    else:
        x_ref, wg_ref, wu_ref, wo_ref, o_ref, acc_ref = refs
    i = pl.program_id(0)
    j = pl.program_id(1)

    @pl.when(j == 0)
    def _():
        acc_ref[...] = jnp.zeros_like(acc_ref)

    @pl.when(i < nu_ref[0])
    def _():
        x = x_ref[...].astype(BF16)
        gate = jnp.dot(x, wg_ref[...], preferred_element_type=F32)
        up = jnp.dot(x, wu_ref[...], preferred_element_type=F32)
        act = (_silu(gate) * up).astype(BF16)
        acc_ref[...] += jnp.dot(act, wo_ref[...], preferred_element_type=F32)

    @pl.when(j == nj - 1)
    def _():
        if has_res:
            o_ref[...] = acc_ref[...] + r_ref[...]
        else:
            o_ref[...] = acc_ref[...]


def _ffn(x, w_in, w_out, tile_expert, n_used, res, tm, tf, name):
    rows, d = x.shape
    f = w_out.shape[1]
    nj = f // tf
    nt = rows // tm

    def col(i, j, nu):
        return jnp.where(i < nu[0], j, nj - 1)

    in_specs = [
        pl.BlockSpec((tm, d), lambda i, j, te, nu: (jnp.minimum(i, nu[0] - 1), 0)),
        pl.BlockSpec((None, d, tf), lambda i, j, te, nu: (te[i], 0, col(i, j, nu))),
        pl.BlockSpec((None, d, tf), lambda i, j, te, nu: (te[i], 0, nj + col(i, j, nu))),
        pl.BlockSpec((None, tf, d), lambda i, j, te, nu: (te[i], col(i, j, nu), 0)),
    ]
    args = [x, w_in, w_in, w_out]
    if res is not None:
        in_specs.append(pl.BlockSpec((tm, d), lambda i, j, te, nu: (i, 0)))
        args.append(res)
    return pl.pallas_call(
        functools.partial(_ffn_body, nj=nj, has_res=res is not None),
        grid_spec=pltpu.PrefetchScalarGridSpec(
            num_scalar_prefetch=2,
            grid=(nt, nj),
            in_specs=in_specs,
            out_specs=pl.BlockSpec((tm, d), lambda i, j, te, nu: (i, 0)),
            scratch_shapes=[pltpu.VMEM((tm, d), F32)]),
        out_shape=jax.ShapeDtypeStruct((rows, d), F32),
        compiler_params=_params(("arbitrary", "arbitrary"), 52),
        name=name,
    )(tile_expert, n_used, *args)


def _router_body(h_ref, wr_ref, o_ref, cnt_ref, carry_ref, *, tm):
    @pl.when(pl.program_id(0) == 0)
    def _():
        carry_ref[...] = jnp.zeros_like(carry_ref)

    logits = jnp.dot(h_ref[...], wr_ref[...], preferred_element_type=F32,
                     precision=lax.Precision.HIGHEST)
    lt = logits.T[:N_EXPERTS, :]
    e = lax.broadcasted_iota(jnp.int32, (N_EXPERTS, tm), 0)
    m1 = jnp.max(lt, axis=0, keepdims=True)
    i1 = jnp.min(jnp.where(lt == m1, e, N_EXPERTS), axis=0, keepdims=True)
    lt2 = jnp.where(e == i1, -jnp.inf, lt)
    m2 = jnp.max(lt2, axis=0, keepdims=True)
    i2 = jnp.min(jnp.where(lt2 == m2, e, N_EXPERTS), axis=0, keepdims=True)
    ex = jnp.exp(m2 - m1)
    g1 = 1.0 / (1.0 + ex)
    g2 = ex / (1.0 + ex)

    oh1 = jnp.where(e == i1, 1.0, 0.0)
    oh2 = jnp.where(e == i2, 1.0, 0.0)
    before = jnp.where(lax.broadcasted_iota(jnp.int32, (tm, tm), 0)
                       < lax.broadcasted_iota(jnp.int32, (tm, tm), 1), 1.0, 0.0).astype(BF16)
    c1 = jnp.dot(oh1.astype(BF16), before, preferred_element_type=F32)
    c2 = jnp.dot(oh2.astype(BF16), before, preferred_element_type=F32)
    t1 = jnp.sum(oh1, axis=1, keepdims=True)
    t2 = jnp.sum(oh2, axis=1, keepdims=True)
    carry = carry_ref[...]
    base = carry[:, :1]
    r1 = jnp.sum(oh1 * (base + c1), axis=0, keepdims=True)
    r2 = jnp.sum(oh2 * (base + t1 + c2), axis=0, keepdims=True)
    carry = carry + t1 + t2
    carry_ref[...] = carry
    cnt_ref[...] = carry

    out = jnp.where(e == 0, i1.astype(F32),
          jnp.where(e == 1, i2.astype(F32),
          jnp.where(e == 2, g1,
          jnp.where(e == 3, g2,
          jnp.where(e == 4, r1,
          jnp.where(e == 5, r2, 0.0))))))
    o_ref[...] = out


def _router(h, w_router, tm=512):
    t, d = h.shape
    wr = jnp.pad(w_router, ((0, 0), (0, LANES - N_EXPERTS)))
    return pl.pallas_call(
        functools.partial(_router_body, tm=tm),
        grid=(t // tm,),
        in_specs=[pl.BlockSpec((tm, d), lambda i: (i, 0)),
                  pl.BlockSpec((d, LANES), lambda i: (0, 0))],
        out_specs=[pl.BlockSpec((N_EXPERTS, tm), lambda i: (0, i)),
                   pl.BlockSpec((N_EXPERTS, LANES), lambda i: (0, 0))],
        out_shape=[jax.ShapeDtypeStruct((N_EXPERTS, t), F32),
                   jax.ShapeDtypeStruct((N_EXPERTS, LANES), F32)],
        scratch_shapes=[pltpu.VMEM((N_EXPERTS, LANES), F32)],
        compiler_params=_params(("arbitrary",), 32),
        name="router_top2",
    )(h, wr)


def _dispatch_body(pos_ref, h_ref, zeros_ref, xs_ref, sem, *, tk, tokens):
    del zeros_ref
    base = pl.program_id(0) * tk

    def row_copy(r, slot):
        dst = pos_ref[slot * tokens + base + r]
        return pltpu.make_async_copy(h_ref.at[pl.ds(r, 1)], xs_ref.at[pl.ds(dst, 1)], sem)

    def start(r, carry):
        row_copy(r, 0).start()
        row_copy(r, 1).start()
        return carry

    lax.fori_loop(0, tk, start, 0, unroll=8)

    def wait(r, carry):
        row_copy(r, 0).wait()
        row_copy(r, 1).wait()
        return carry

    lax.fori_loop(0, tk, wait, 0, unroll=8)


def _dispatch(h, pos_flat, rows, tk=256):
    t, d = h.shape
    zeros = jnp.zeros((rows, d), h.dtype)
    return pl.pallas_call(
        functools.partial(_dispatch_body, tk=tk, tokens=t),
        grid_spec=pltpu.PrefetchScalarGridSpec(
            num_scalar_prefetch=1,
            grid=(t // tk,),
            in_specs=[pl.BlockSpec((tk, d), lambda i, pos: (i, 0)),
                      pl.BlockSpec(memory_space=pl.ANY)],
            out_specs=pl.BlockSpec(memory_space=pl.ANY),
            scratch_shapes=[pltpu.SemaphoreType.DMA(())]),
        out_shape=jax.ShapeDtypeStruct((rows, d), h.dtype),
        input_output_aliases={2: 0},
        compiler_params=_params(("arbitrary",), 32),
        name="moe_dispatch",
    )(pos_flat, h, zeros)


def _combine_body(pos_ref, x_ref, gates_ref, gf_ref, y_ref, o_ref, ybuf, sem, *, tk, tokens):
    base = pl.program_id(0) * tk

    def row_copy(r, slot):
        src = pos_ref[slot * tokens + base + r]
        return pltpu.make_async_copy(y_ref.at[pl.ds(src, 1)], ybuf.at[slot, pl.ds(r, 1)], sem)

    def start(r, carry):
        row_copy(r, 0).start()
        row_copy(r, 1).start()
        return carry

    lax.fori_loop(0, tk, start, 0, unroll=8)

    def wait(r, carry):
        row_copy(r, 0).wait()
        row_copy(r, 1).wait()
        return carry

    lax.fori_loop(0, tk, wait, 0, unroll=8)

    gates = gates_ref[...]
    x = x_ref[...] + gates[:, 0:1] * ybuf[0] + gates[:, 1:2] * ybuf[1]
    ms = jnp.mean(x * x, axis=-1, keepdims=True)
    o_ref[...] = x * lax.rsqrt(ms + EPS) * gf_ref[...]


def _combine(x2, y, pos_flat, gates, g_final, tk=256):
    t, d = x2.shape
    return pl.pallas_call(
        functools.partial(_combine_body, tk=tk, tokens=t),
        grid_spec=pltpu.PrefetchScalarGridSpec(
            num_scalar_prefetch=1,
            grid=(t // tk,),
            in_specs=[pl.BlockSpec((tk, d), lambda i, pos: (i, 0)),
                      pl.BlockSpec((tk, 2), lambda i, pos: (i, 0)),
                      pl.BlockSpec((1, d), lambda i, pos: (0, 0)),
                      pl.BlockSpec(memory_space=pl.ANY)],
            out_specs=pl.BlockSpec((tk, d), lambda i, pos: (i, 0)),
            scratch_shapes=[pltpu.VMEM((2, tk, d), F32),
                            pltpu.SemaphoreType.DMA(())]),
        out_shape=jax.ShapeDtypeStruct((t, d), F32),
        compiler_params=_params(("arbitrary",), 32),
        name="moe_combine",
    )(pos_flat, x2, gates, g_final.reshape(1, d), y)


MOE_TM = 512
FFN_TF = 512


def _routing_tables(route, counts, n_tiles):
    idx1 = route[0].astype(jnp.int32)
    idx2 = route[1].astype(jnp.int32)
    cnt = counts[:, 0].astype(jnp.int32)
    tiles_e = (cnt + MOE_TM - 1) // MOE_TM
    tile_end = jnp.cumsum(tiles_e)
    offset = (tile_end - tiles_e) * MOE_TM
    pos1 = offset[idx1] + route[4].astype(jnp.int32)
    pos2 = offset[idx2] + route[5].astype(jnp.int32)
    n_used = tile_end[-1]
    tiles = jnp.arange(n_tiles, dtype=jnp.int32)
    te = jnp.sum((tiles[:, None] >= tile_end[None, :]).astype(jnp.int32), axis=1)
    te = jnp.minimum(te, N_EXPERTS - 1)
    te = jnp.where(tiles < n_used, te, te[n_used - 1])
    pos_flat = jnp.concatenate([pos1, pos2]).astype(jnp.int32)
    gates = jnp.stack([route[2], route[3]], axis=1)
    return pos_flat, gates, te, n_used.reshape(1).astype(jnp.int32)


def kernel(x, ln_mix, ln_ffn, ln_final, ret_w_in, ret_w_out, ffn_w_in, ffn_w_out,
           att_w_in, att_w_out, att_rel_bias, moe_router, moe_w_in, moe_w_out):
    b, s, d = x.shape
    t = b * s
    x2 = x.reshape(t, d)

    h = _rmsnorm(x2, ln_mix[0], BF16)
    proj = _matmul(h, ret_w_in[0], None, BF16, 1024, 1024, "ret_proj_in")
    pos = jnp.arange(s, dtype=F32)
    inv_freq = ROPE_BASE ** (-jnp.arange(0, RET_QK_DIM, 2, dtype=F32) / RET_QK_DIM)
    ang = pos[:, None] * inv_freq[None, :]
    y = _retention(proj.reshape(b, s, -1), jnp.cos(ang), jnp.sin(ang))
    x2 = _matmul(y.reshape(t, RET_HV), ret_w_out[0], x2, F32, 1024, 512, "ret_proj_out")

    h = _rmsnorm(x2, ln_ffn[0], BF16)
    dense_tiles = t // MOE_TM
    x2 = _ffn(h, ffn_w_in.astype(BF16), ffn_w_out.astype(BF16),
              jnp.zeros((dense_tiles,), jnp.int32), jnp.full((1,), dense_tiles, jnp.int32),
              x2, MOE_TM, FFN_TF, "dense_swiglu")

    h = _rmsnorm(x2, ln_mix[1], BF16)
    qkv = _matmul(h, att_w_in[0], None, BF16, 1024, 1024, "att_proj_in")
    a = _attention(qkv.reshape(b, s, -1), att_rel_bias[0])
    x2 = _matmul(a.reshape(t, d), att_w_out[0], x2, F32, 1024, 1024, "att_proj_out")

    h = _rmsnorm(x2, ln_ffn[1], F32)
    route, counts = _router(h, moe_router[0])
    n_tiles = 2 * t // MOE_TM + N_EXPERTS
    pos_flat, gates, te, n_used = _routing_tables(route, counts, n_tiles)
    xs = _dispatch(h, pos_flat, n_tiles * MOE_TM)
    ys = _ffn(xs, moe_w_in[0].astype(BF16), moe_w_out[0].astype(BF16), te, n_used,
              None, MOE_TM, FFN_TF, "expert_swiglu")
    out = _combine(x2, ys, pos_flat, gates, ln_final)
    return out.reshape(b, s, d)
```

```python
import functools
import math
from typing import NamedTuple, Optional

import jax
import jax.numpy as jnp
from jax import lax
from jax.experimental import pallas as pl
from jax.experimental.pallas import tpu as pltpu

D_MODEL = 2048
CHUNK = 64
EPS = 1e-6
RET_HEADS = 8
RET_QK_DIM = D_MODEL // RET_HEADS
RET_V_DIM = 2 * RET_QK_DIM
RET_HV = RET_HEADS * RET_V_DIM
ROPE_BASE = 10000.0
ATT_HEADS = 16
ATT_HEAD_DIM = D_MODEL // ATT_HEADS
LEFT_CHUNKS = 8
REL_CLIP = 128
NEG_INF = -1e30
N_EXPERTS = 8

LANES = 128
MIB = 1024 * 1024

F32 = jnp.float32
BF16 = jnp.bfloat16


def _params(semantics, vmem_mib):
    return pltpu.CompilerParams(
        dimension_semantics=semantics, vmem_limit_bytes=vmem_mib * MIB)


def _silu(x):
    return x * (1.0 / (1.0 + jnp.exp(-x)))


class _SideCast(NamedTuple):
    src: jax.Array
    num_blocks: int
    block_cols: int
    col_block: int = 0


def _call_with_sides(body, *, grid, in_specs, args, out_specs, out_shapes, scratch_shapes,
                     sides, params, name, prefetch=()):
    steps = math.prod(grid)
    n_pre, n_in, n_out, n_side = len(prefetch), len(args), len(out_shapes), len(sides)
    in_specs, args = list(in_specs), list(args)
    out_specs, out_shapes = list(out_specs), list(out_shapes)

    def block_spec(side, col_block):
        def index(*ids):
            step = ids[0]
            for extent, pid in zip(grid[1:], ids[1:len(grid)]):
                step = step * extent + pid
            return ((step * side.num_blocks) // steps, col_block)
        return pl.BlockSpec((side.src.shape[0] // side.num_blocks, side.block_cols), index)

    for side in sides:
        assert side.num_blocks <= steps and side.src.shape[0] % side.num_blocks == 0
        in_specs.append(block_spec(side, side.col_block))
        args.append(side.src)
        out_specs.append(block_spec(side, 0))
        out_shapes.append(jax.ShapeDtypeStruct((side.src.shape[0], side.block_cols), BF16))

    def wrapped(*refs):
        pre, refs = refs[:n_pre], refs[n_pre:]
        ins = refs[:n_in]
        srcs = refs[n_in:n_in + n_side]
        outs = refs[n_in + n_side:n_in + n_side + n_out]
        side_outs = refs[n_in + n_side + n_out:n_in + 2 * n_side + n_out]
        scratch = refs[n_in + 2 * n_side + n_out:]
        for src_ref, dst_ref in zip(srcs, side_outs):
            dst_ref[...] = src_ref[...].astype(BF16)
        body(*pre, *ins, *outs, *scratch)

    res = pl.pallas_call(
        wrapped,
        grid_spec=pltpu.PrefetchScalarGridSpec(
            num_scalar_prefetch=n_pre, grid=grid, in_specs=in_specs, out_specs=out_specs,
            scratch_shapes=scratch_shapes),
        out_shape=out_shapes, compiler_params=params, name=name)(*prefetch, *args)
    return res[:n_out], res[n_out:]


def _rmsnorm_body(x_ref, g_ref, o_ref):
    x = x_ref[...]
    ms = jnp.mean(x * x, axis=-1, keepdims=True)
    o_ref[...] = (x * lax.rsqrt(ms + EPS) * g_ref[...]).astype(o_ref.dtype)


def _rmsnorm(x2, g, out_dtype, tm=512):
    t, d = x2.shape
    return pl.pallas_call(
        _rmsnorm_body,
        grid=(t // tm,),
        in_specs=[pl.BlockSpec((tm, d), lambda i: (i, 0)),
                  pl.BlockSpec((1, d), lambda i: (0, 0))],
        out_specs=pl.BlockSpec((tm, d), lambda i: (i, 0)),
        out_shape=jax.ShapeDtypeStruct((t, d), out_dtype),
        compiler_params=_params(("arbitrary",), 32),
        name="rmsnorm",
    )(x2, g.reshape(1, d))


def _matmul_body(*refs, has_res):
    if has_res:
        x_ref, w_ref, r_ref, o_ref, wb_ref = refs
    else:
        x_ref, w_ref, o_ref, wb_ref = refs

    @pl.when(pl.program_id(1) == 0)
    def _():
        wb_ref[...] = w_ref[...].astype(BF16)

    acc = jnp.dot(x_ref[...], wb_ref[...], preferred_element_type=F32)
    if has_res:
        acc = acc + r_ref[...]
    o_ref[...] = acc.astype(o_ref.dtype)


def _matmul(x, w, res, out_dtype, tm, tn, name, sides=()):
    m, k = x.shape
    n = w.shape[1]
    in_specs = [pl.BlockSpec((tm, k), lambda j, i: (i, 0)),
                pl.BlockSpec((k, tn), lambda j, i: (0, j))]
    args = [x, w]
    if res is not None:
        in_specs.append(pl.BlockSpec((tm, tn), lambda j, i: (i, j)))
        args.append(res)
    (out,), side_outs = _call_with_sides(
        functools.partial(_matmul_body, has_res=res is not None),
        grid=(n // tn, m // tm),
        in_specs=in_specs, args=args,
        out_specs=[pl.BlockSpec((tm, tn), lambda j, i: (i, j))],
        out_shapes=[jax.ShapeDtypeStruct((m, n), out_dtype)],
        scratch_shapes=[pltpu.VMEM((k, tn), BF16)],
        sides=sides,
        params=_params(("arbitrary", "arbitrary"), 52),
        name=name)
    return out, side_outs


def _retention_body(q_ref, k_ref, v_ref, g_ref, cos_ref, sin_ref, o_ref, state_ref,
                    *, block_rows, seq):
    L = block_rows
    half = RET_QK_DIM // 2
    head = pl.program_id(1)

    def log_gamma(rows):
        headf = (jnp.zeros((rows, 1), jnp.int32) + head).astype(F32)
        return jnp.log(1.0 - jnp.exp2(-5.0 - headf))

    log_g = log_gamma(L)
    n_i = lax.broadcasted_iota(jnp.int32, (L, L), 0)
    m_i = lax.broadcasted_iota(jnp.int32, (L, L), 1)
    dist = jnp.abs(n_i - m_i).astype(F32)
    visible = (m_i >> 6) <= (n_i >> 6)
    decay = jnp.where(visible, jnp.exp(log_g * dist), 0.0)
    r = lax.broadcasted_iota(jnp.int32, (L, 1), 0).astype(F32)
    q_decay = jnp.exp(log_g * (r + 1.0))
    k_decay = jnp.exp(log_g * (L - 1.0 - r))
    block_decay = jnp.exp(log_gamma(RET_QK_DIM) * float(L))
    scale = RET_QK_DIM ** -0.5

    state_ref[...] = jnp.zeros_like(state_ref)

    def body(s, carry):
        r0 = pl.multiple_of(s * L, L)
        cos = cos_ref[pl.ds(r0, L), :]
        sin = sin_ref[pl.ds(r0, L), :]

        def rot(t):
            t1 = t[:, :half]
            t2 = t[:, half:]
            return jnp.concatenate([t1 * cos - t2 * sin, t1 * sin + t2 * cos], axis=1)

        q = rot(q_ref[pl.ds(r0, L), :].astype(F32)).astype(BF16)
        kf = rot(k_ref[pl.ds(r0, L), :].astype(F32)) * scale
        v = v_ref[pl.ds(r0, L), :]
        s_qk = lax.dot_general(q, kf.astype(BF16), (((1,), (1,)), ((), ())),
                               preferred_element_type=F32) * decay
        inner = jnp.dot(s_qk.astype(BF16), v, preferred_element_type=F32)
        st = state_ref[...]
        cross = jnp.dot(q, st.astype(BF16), preferred_element_type=F32) * q_decay
        kd_t = (kf * k_decay).T.astype(BF16)
        state_ref[...] = st * block_decay + jnp.dot(kd_t, v, preferred_element_type=F32)
        o = inner + cross
        o = o * lax.rsqrt(jnp.mean(o * o, axis=-1, keepdims=True) + EPS)
        gate = g_ref[pl.ds(r0, L), :].astype(F32)
        o_ref[pl.ds(r0, L), :] = (_silu(gate) * o).astype(o_ref.dtype)
        return carry

    lax.fori_loop(0, seq // L, body, 0, unroll=2)


def _retention(proj, cos, sin, sides=(), block_rows=256):
    b, s, _ = proj.shape
    nq = D_MODEL // RET_QK_DIM
    nv = RET_HV // RET_V_DIM
    (out,), side_outs = _call_with_sides(
        functools.partial(_retention_body, block_rows=block_rows, seq=s),
        grid=(b, RET_HEADS),
        in_specs=[
            pl.BlockSpec((None, s, RET_QK_DIM), lambda i, h: (i, 0, h)),
            pl.BlockSpec((None, s, RET_QK_DIM), lambda i, h: (i, 0, nq + h)),
            pl.BlockSpec((None, s, RET_V_DIM), lambda i, h: (i, 0, nv + h)),
            pl.BlockSpec((None, s, RET_V_DIM), lambda i, h: (i, 0, 2 * nv + h)),
            pl.BlockSpec((s, RET_QK_DIM // 2), lambda i, h: (0, 0)),
            pl.BlockSpec((s, RET_QK_DIM // 2), lambda i, h: (0, 0)),
        ],
        args=[proj, proj, proj, proj, cos, sin],
        out_specs=[pl.BlockSpec((None, s, RET_V_DIM), lambda i, h: (i, 0, h))],
        out_shapes=[jax.ShapeDtypeStruct((b, s, RET_HV), BF16)],
        scratch_shapes=[pltpu.VMEM((RET_QK_DIM, RET_V_DIM), F32)],
        sides=sides,
        params=_params(("arbitrary", "arbitrary"), 48),
        name="retention_core")
    return out, side_outs


ATT_PAD = LEFT_CHUNKS * CHUNK
ATT_LQ = 256
ATT_WIN = ATT_LQ + ATT_PAD
ATT_ROLL_W = ATT_LQ + ATT_WIN
REL_PAD = 384


def _attention_body(q_ref, k_ref, v_ref, rb_ref, o_ref, bm_ref, *, seq):
    @pl.when(pl.program_id(1) == 0)
    def _():
        t = lax.broadcasted_iota(jnp.int32, (REL_PAD, ATT_ROLL_W), 1)
        j = lax.broadcasted_iota(jnp.int32, (REL_PAD, ATT_ROLL_W), 0)
        idx = jnp.where(t <= ATT_WIN, jnp.clip(ATT_PAD - t, -REL_CLIP, REL_CLIP),
                        REL_CLIP) + REL_CLIP
        onehot = jnp.where(idx == j, 1.0, 0.0).astype(BF16)
        rb = rb_ref[...]
        hi = rb.astype(BF16)
        rest = rb - hi.astype(F32)
        mid = rest.astype(BF16)
        lo = (rest - mid.astype(F32)).astype(BF16)
        g = (jnp.dot(hi, onehot, preferred_element_type=F32)
             + jnp.dot(mid, onehot, preferred_element_type=F32)
             + jnp.dot(lo, onehot, preferred_element_type=F32))
        rows = jnp.broadcast_to(g[0:1, :], (ATT_LQ, ATT_ROLL_W))
        row = lax.broadcasted_iota(jnp.int32, (ATT_LQ, ATT_ROLL_W), 0)
        shift = 1
        while shift < ATT_LQ:
            rows = jnp.where((row & shift) != 0, pltpu.roll(rows, shift, 1), rows)
            shift *= 2
        qc = lax.broadcasted_iota(jnp.int32, (ATT_LQ, ATT_WIN), 0) >> 6
        kc = lax.broadcasted_iota(jnp.int32, (ATT_LQ, ATT_WIN), 1) >> 6
        ahead = kc - qc
        banded = jnp.where(ahead <= LEFT_CHUNKS, rows[:, :ATT_WIN], NEG_INF)
        bm_ref[...] = jnp.where(ahead >= 0, banded, NEG_INF)

    scale = ATT_HEAD_DIM ** -0.5
    for q0 in range(0, seq, ATT_LQ):
        k_lo = max(0, q0 - ATT_PAD)
        width = q0 + ATT_LQ - k_lo
        q = q_ref[q0:q0 + ATT_LQ, :]
        kw = k_ref[k_lo:k_lo + width, :]
        vw = v_ref[k_lo:k_lo + width, :]
        logits = lax.dot_general(q, kw, (((1,), (1,)), ((), ())),
                                 preferred_element_type=F32) * scale
        logits = logits + bm_ref[:, ATT_WIN - width:]
        mx = jnp.max(logits, axis=-1, keepdims=True)
        p = jnp.exp(logits - mx)
        denom = jnp.sum(p, axis=-1, keepdims=True)
        o = jnp.dot(p.astype(BF16), vw, preferred_element_type=F32) / denom
        o_ref[q0:q0 + ATT_LQ, :] = o.astype(o_ref.dtype)


def _attention(qkv, rel_bias, sides=()):
    b, s, _ = qkv.shape
    rb = jnp.zeros((ATT_HEADS, 8, REL_PAD), F32).at[:, 0, :2 * REL_CLIP + 1].set(rel_bias)
    (out,), side_outs = _call_with_sides(
        functools.partial(_attention_body, seq=s),
        grid=(ATT_HEADS, b),
        in_specs=[
            pl.BlockSpec((None, s, ATT_HEAD_DIM), lambda h, i: (i, 0, h)),
            pl.BlockSpec((None, s, ATT_HEAD_DIM), lambda h, i: (i, 0, ATT_HEADS + h)),
            pl.BlockSpec((None, s, ATT_HEAD_DIM), lambda h, i: (i, 0, 2 * ATT_HEADS + h)),
            pl.BlockSpec((None, 8, REL_PAD), lambda h, i: (h, 0, 0)),
        ],
        args=[qkv, qkv, qkv, rb],
        out_specs=[pl.BlockSpec((None, s, ATT_HEAD_DIM), lambda h, i: (i, 0, h))],
        out_shapes=[jax.ShapeDtypeStruct((b, s, D_MODEL), BF16)],
        scratch_shapes=[pltpu.VMEM((ATT_LQ, ATT_WIN), F32)],
        sides=sides,
        params=_params(("arbitrary", "arbitrary"), 48),
        name="chunk_attention")
    return out, side_outs


def _ffn_body(te_ref, nu_ref, *refs, normalize):
    del te_ref
    if normalize:
        x_ref, g_ref, wg_ref, wu_ref, wo_ref, o_ref, xb_ref = refs
    else:
        x_ref, wg_ref, wu_ref, wo_ref, o_ref, xb_ref = refs

    @pl.when(pl.program_id(1) == 0)
    def _():
        x = x_ref[...]
        if normalize:
            ms = jnp.mean(x * x, axis=-1, keepdims=True)
            xb_ref[...] = (x * lax.rsqrt(ms + EPS) * g_ref[...]).astype(BF16)
            o_ref[...] = x
        else:
            xb_ref[...] = x.astype(BF16)
            o_ref[...] = jnp.zeros_like(o_ref)

    @pl.when(pl.program_id(0) < nu_ref[0])
    def _():
        xb = xb_ref[...]
        gate = jnp.dot(xb, wg_ref[...], preferred_element_type=F32)
        up = jnp.dot(xb, wu_ref[...], preferred_element_type=F32)
        act = (_silu(gate) * up).astype(BF16)
        o_ref[...] += jnp.dot(act, wo_ref[...], preferred_element_type=F32)


def _ffn(x, gain, w_gate, w_up, w_out, tile_expert, n_used, tm, tf, vmem_mib, name, sides=()):
    rows, d = x.shape
    f = w_out.shape[1]
    nj = f // tf
    nt = rows // tm
    normalize = gain is not None

    def col(i, j, nu):
        return jnp.where(i < nu[0], j, nj - 1)

    in_specs = [pl.BlockSpec((tm, d), lambda i, j, te, nu: (jnp.minimum(i, nu[0] - 1), 0))]
    args = [x]
    if normalize:
        in_specs.append(pl.BlockSpec((1, d), lambda i, j, te, nu: (0, 0)))
        args.append(gain.reshape(1, d))
    in_specs += [
        pl.BlockSpec((None, d, tf), lambda i, j, te, nu: (te[i], 0, col(i, j, nu))),
        pl.BlockSpec((None, d, tf), lambda i, j, te, nu: (te[i], 0, col(i, j, nu))),
        pl.BlockSpec((None, tf, d), lambda i, j, te, nu: (te[i], col(i, j, nu), 0)),
    ]
    args += [w_gate, w_up, w_out]
    (out,), side_outs = _call_with_sides(
        functools.partial(_ffn_body, normalize=normalize),
        grid=(nt, nj),
        in_specs=in_specs, args=args,
        out_specs=[pl.BlockSpec((tm, d), lambda i, j, te, nu: (i, 0))],
        out_shapes=[jax.ShapeDtypeStruct((rows, d), F32)],
        scratch_shapes=[pltpu.VMEM((tm, d), BF16)],
        sides=sides,
        params=_params(("arbitrary", "arbitrary"), vmem_mib),
        name=name,
        prefetch=(tile_expert, n_used))
    return out, side_outs


def _router_body(h_ref, wr_ref, o_ref, cnt_ref, carry_ref, *, tm):
    @pl.when(pl.program_id(0) == 0)
    def _():
        carry_ref[...] = jnp.zeros_like(carry_ref)

    logits = jnp.dot(h_ref[...], wr_ref[...], preferred_element_type=F32,
                     precision=lax.Precision.HIGHEST)
    lt = logits.T[:N_EXPERTS, :]
    e = lax.broadcasted_iota(jnp.int32, (N_EXPERTS, tm), 0)
    m1 = jnp.max(lt, axis=0, keepdims=True)
    i1 = jnp.min(jnp.where(lt == m1, e, N_EXPERTS), axis=0, keepdims=True)
    lt2 = jnp.where(e == i1, -jnp.inf, lt)
    m2 = jnp.max(lt2, axis=0, keepdims=True)
    i2 = jnp.min(jnp.where(lt2 == m2, e, N_EXPERTS), axis=0, keepdims=True)
    ex = jnp.exp(m2 - m1)
    g1 = 1.0 / (1.0 + ex)
    g2 = ex / (1.0 + ex)

    oh1 = jnp.where(e == i1, 1.0, 0.0)
    oh2 = jnp.where(e == i2, 1.0, 0.0)
    before = jnp.where(lax.broadcasted_iota(jnp.int32, (tm, tm), 0)
                       < lax.broadcasted_iota(jnp.int32, (tm, tm), 1), 1.0, 0.0).astype(BF16)
    c1 = jnp.dot(oh1.astype(BF16), before, preferred_element_type=F32)
    c2 = jnp.dot(oh2.astype(BF16), before, preferred_element_type=F32)
    t1 = jnp.sum(oh1, axis=1, keepdims=True)
    t2 = jnp.sum(oh2, axis=1, keepdims=True)
    carry = carry_ref[...]
    base = carry[:, :1]
    r1 = jnp.sum(oh1 * (base + c1), axis=0, keepdims=True)
    r2 = jnp.sum(oh2 * (base + t1 + c2), axis=0, keepdims=True)
    carry = carry + t1 + t2
    carry_ref[...] = carry
    cnt_ref[...] = carry

    out = jnp.where(e == 0, i1.astype(F32),
          jnp.where(e == 1, i2.astype(F32),
          jnp.where(e == 2, g1,
          jnp.where(e == 3, g2,
          jnp.where(e == 4, r1,
          jnp.where(e == 5, r2, 0.0))))))
    o_ref[...] = out


def _router(h, w_router, tm=512):
    t, d = h.shape
    wr = jnp.pad(w_router, ((0, 0), (0, LANES - N_EXPERTS)))
    return pl.pallas_call(
        functools.partial(_router_body, tm=tm),
        grid=(t // tm,),
        in_specs=[pl.BlockSpec((tm, d), lambda i: (i, 0)),
                  pl.BlockSpec((d, LANES), lambda i: (0, 0))],
        out_specs=[pl.BlockSpec((N_EXPERTS, tm), lambda i: (0, i)),
                   pl.BlockSpec((N_EXPERTS, LANES), lambda i: (0, 0))],
        out_shape=[jax.ShapeDtypeStruct((N_EXPERTS, t), F32),
                   jax.ShapeDtypeStruct((N_EXPERTS, LANES), F32)],
        scratch_shapes=[pltpu.VMEM((N_EXPERTS, LANES), F32)],
        compiler_params=_params(("arbitrary",), 32),
        name="router_top2",
    )(h, wr)


def _dispatch_body(pos_ref, h_ref, zeros_ref, xs_ref, sem, *, tk, tokens):
    del zeros_ref
    base = pl.program_id(0) * tk

    def row_copy(r, slot):
        dst = pos_ref[slot * tokens + base + r]
        return pltpu.make_async_copy(h_ref.at[pl.ds(r, 1)], xs_ref.at[pl.ds(dst, 1)], sem)

    def start(r, carry):
        row_copy(r, 0).start()
        row_copy(r, 1).start()
        return carry

    lax.fori_loop(0, tk, start, 0, unroll=8)

    def wait(r, carry):
        row_copy(r, 0).wait()
        row_copy(r, 1).wait()
        return carry

    lax.fori_loop(0, tk, wait, 0, unroll=8)


def _dispatch(h, pos_flat, rows, tk=256):
    t, d = h.shape
    zeros = jnp.zeros((rows, d), h.dtype)
    return pl.pallas_call(
        functools.partial(_dispatch_body, tk=tk, tokens=t),
        grid_spec=pltpu.PrefetchScalarGridSpec(
            num_scalar_prefetch=1,
            grid=(t // tk,),
            in_specs=[pl.BlockSpec((tk, d), lambda i, pos: (i, 0)),
                      pl.BlockSpec(memory_space=pl.ANY)],
            out_specs=pl.BlockSpec(memory_space=pl.ANY),
            scratch_shapes=[pltpu.SemaphoreType.DMA(())]),
        out_shape=jax.ShapeDtypeStruct((rows, d), h.dtype),
        input_output_aliases={2: 0},
        compiler_params=_params(("arbitrary",), 32),
        name="moe_dispatch",
    )(pos_flat, h, zeros)


def _combine_body(pos_ref, x_ref, gates_ref, gf_ref, y_ref, o_ref, ybuf, sem, *, tk, tokens):
    base = pl.program_id(0) * tk

    def row_copy(r, slot):
        src = pos_ref[slot * tokens + base + r]
        return pltpu.make_async_copy(y_ref.at[pl.ds(src, 1)], ybuf.at[slot, pl.ds(r, 1)], sem)

    def start(r, carry):
        row_copy(r, 0).start()
        row_copy(r, 1).start()
        return carry

    lax.fori_loop(0, tk, start, 0, unroll=8)

    def wait(r, carry):
        row_copy(r, 0).wait()
        row_copy(r, 1).wait()
        return carry

    lax.fori_loop(0, tk, wait, 0, unroll=8)

    gates = gates_ref[...]
    x = x_ref[...] + gates[:, 0:1] * ybuf[0] + gates[:, 1:2] * ybuf[1]
    ms = jnp.mean(x * x, axis=-1, keepdims=True)
    o_ref[...] = x * lax.rsqrt(ms + EPS) * gf_ref[...]


def _combine(x2, y, pos_flat, gates, g_final, tk=256):
    t, d = x2.shape
    return pl.pallas_call(
        functools.partial(_combine_body, tk=tk, tokens=t),
        grid_spec=pltpu.PrefetchScalarGridSpec(
            num_scalar_prefetch=1,
            grid=(t // tk,),
            in_specs=[pl.BlockSpec((tk, d), lambda i, pos: (i, 0)),
                      pl.BlockSpec((tk, 2), lambda i, pos: (i, 0)),
                      pl.BlockSpec((1, d), lambda i, pos: (0, 0)),
                      pl.BlockSpec(memory_space=pl.ANY)],
            out_specs=pl.BlockSpec((tk, d), lambda i, pos: (i, 0)),
            scratch_shapes=[pltpu.VMEM((2, tk, d), F32),
                            pltpu.SemaphoreType.DMA(())]),
        out_shape=jax.ShapeDtypeStruct((t, d), F32),
        compiler_params=_params(("arbitrary",), 32),
        name="moe_combine",
    )(pos_flat, x2, gates, g_final.reshape(1, d), y)


MOE_TM = 512
DENSE_TF = 512
EXPERT_TF = 1024
PROJ_TM = 1024
PROJ_TN = 1024


def _routing_tables(route, counts, n_tiles):
    idx1 = route[0].astype(jnp.int32)
    idx2 = route[1].astype(jnp.int32)
    cnt = counts[:, 0].astype(jnp.int32)
    tiles_e = (cnt + MOE_TM - 1) // MOE_TM
    tile_end = jnp.cumsum(tiles_e)
    offset = (tile_end - tiles_e) * MOE_TM
    pos1 = offset[idx1] + route[4].astype(jnp.int32)
    pos2 = offset[idx2] + route[5].astype(jnp.int32)
    n_used = tile_end[-1]
    tiles = jnp.arange(n_tiles, dtype=jnp.int32)
    te = jnp.sum((tiles[:, None] >= tile_end[None, :]).astype(jnp.int32), axis=1)
    te = jnp.minimum(te, N_EXPERTS - 1)
    te = jnp.where(tiles < n_used, te, te[n_used - 1])
    pos_flat = jnp.concatenate([pos1, pos2]).astype(jnp.int32)
    gates = jnp.stack([route[2], route[3]], axis=1)
    return pos_flat, gates, te, n_used.reshape(1).astype(jnp.int32)


def kernel(x, ln_mix, ln_ffn, ln_final, ret_w_in, ret_w_out, ffn_w_in, ffn_w_out,
           att_w_in, att_w_out, att_rel_bias, moe_router, moe_w_in, moe_w_out):
    b, s, d = x.shape
    t = b * s
    x2 = x.reshape(t, d)
    f_dense = ffn_w_out.shape[1]
    f_expert = moe_w_out.shape[2]

    moe_in2 = moe_w_in[0].reshape(N_EXPERTS * d, 2 * f_expert)
    moe_out2 = moe_w_out[0].reshape(N_EXPERTS * f_expert, d)
    ret_steps = b * RET_HEADS
    att_steps = ATT_HEADS * b
    cast_moe_gate = _SideCast(moe_in2, 128, f_expert, 0)
    cast_moe_up = _SideCast(moe_in2, att_steps, f_expert, 1)
    cast_moe_out = _SideCast(moe_out2, 256, d)
    cast_ffn_gate = _SideCast(ffn_w_in[0], ret_steps, f_dense, 0)
    cast_ffn_up = _SideCast(ffn_w_in[0], ret_steps, f_dense, 1)
    cast_ffn_out = _SideCast(ffn_w_out[0], ret_steps // 2, d)

    h = _rmsnorm(x2, ln_mix[0], BF16)
    proj, (moe_gate_bf,) = _matmul(h, ret_w_in[0], None, BF16, PROJ_TM, PROJ_TN, "ret_proj_in",
                                   sides=[cast_moe_gate])
    pos = jnp.arange(s, dtype=F32)
    inv_freq = ROPE_BASE ** (-jnp.arange(0, RET_QK_DIM, 2, dtype=F32) / RET_QK_DIM)
    ang = pos[:, None] * inv_freq[None, :]
    y, (ffn_gate_bf, ffn_up_bf, ffn_out_bf) = _retention(
        proj.reshape(b, s, -1), jnp.cos(ang), jnp.sin(ang),
        sides=[cast_ffn_gate, cast_ffn_up, cast_ffn_out])
    x2, _ = _matmul(y.reshape(t, RET_HV), ret_w_out[0], x2, F32, PROJ_TM, PROJ_TN // 2,
                    "ret_proj_out")

    dense_tiles = t // MOE_TM
    x2, (moe_out_bf,) = _ffn(
        x2, ln_ffn[0], ffn_gate_bf[None], ffn_up_bf[None], ffn_out_bf[None],
        jnp.zeros((dense_tiles,), jnp.int32), jnp.full((1,), dense_tiles, jnp.int32),
        MOE_TM, DENSE_TF, 48, "dense_swiglu", sides=[cast_moe_out])

    h = _rmsnorm(x2, ln_mix[1], BF16)
    qkv, _ = _matmul(h, att_w_in[0], None, BF16, PROJ_TM, PROJ_TN, "att_proj_in")
    a, (moe_up_bf,) = _attention(qkv.reshape(b, s, -1), att_rel_bias[0], sides=[cast_moe_up])
    x2, _ = _matmul(a.reshape(t, d), att_w_out[0], x2, F32, PROJ_TM, PROJ_TN, "att_proj_out")

    h = _rmsnorm(x2, ln_ffn[1], F32)
    route, counts = _router(h, moe_router[0])
    n_tiles = 2 * t // MOE_TM + N_EXPERTS
    pos_flat, gates, te, n_used = _routing_tables(route, counts, n_tiles)
    xs = _dispatch(h, pos_flat, n_tiles * MOE_TM)
    ys, _ = _ffn(xs, None, moe_gate_bf.reshape(N_EXPERTS, d, f_expert),
                 moe_up_bf.reshape(N_EXPERTS, d, f_expert),
                 moe_out_bf.reshape(N_EXPERTS, f_expert, d), te, n_used,
                 MOE_TM, EXPERT_TF, 56, "expert_swiglu")
    out = _combine(x2, ys, pos_flat, gates, ln_final)
    return out.reshape(b, s, d)
```

```python
import functools
import math
from typing import NamedTuple

import jax
import jax.numpy as jnp
from jax import lax
from jax.experimental import pallas as pl
from jax.experimental.pallas import tpu as pltpu

D_MODEL = 2048
CHUNK = 64
EPS = 1e-6
RET_HEADS = 8
RET_QK_DIM = D_MODEL // RET_HEADS
RET_V_DIM = 2 * RET_QK_DIM
RET_HV = RET_HEADS * RET_V_DIM
ROPE_BASE = 10000.0
ATT_HEADS = 16
ATT_HEAD_DIM = D_MODEL // ATT_HEADS
LEFT_CHUNKS = 8
REL_CLIP = 128
NEG_INF = -1e30
N_EXPERTS = 8

LANES = 128
MIB = 1024 * 1024

F32 = jnp.float32
BF16 = jnp.bfloat16


def _params(semantics, vmem_mib):
    return pltpu.CompilerParams(
        dimension_semantics=semantics, vmem_limit_bytes=vmem_mib * MIB)


def _silu(x):
    return x * (1.0 / (1.0 + jnp.exp(-x)))


class _SideCast(NamedTuple):
    src: jax.Array
    num_blocks: int
    block_cols: int
    col_block: int = 0


def _call_with_sides(body, *, grid, in_specs, args, out_specs, out_shapes, scratch_shapes,
                     sides, params, name, prefetch=()):
    steps = math.prod(grid)
    n_pre, n_in, n_out, n_side = len(prefetch), len(args), len(out_shapes), len(sides)
    in_specs, args = list(in_specs), list(args)
    out_specs, out_shapes = list(out_specs), list(out_shapes)

    def block_spec(side, col_block):
        def index(*ids):
            step = ids[0]
            for extent, pid in zip(grid[1:], ids[1:len(grid)]):
                step = step * extent + pid
            return ((step * side.num_blocks) // steps, col_block)
        return pl.BlockSpec((side.src.shape[0] // side.num_blocks, side.block_cols), index)

    for side in sides:
        assert side.num_blocks <= steps and side.src.shape[0] % side.num_blocks == 0
        in_specs.append(block_spec(side, side.col_block))
        args.append(side.src)
        out_specs.append(block_spec(side, 0))
        out_shapes.append(jax.ShapeDtypeStruct((side.src.shape[0], side.block_cols), BF16))

    def wrapped(*refs):
        pre, refs = refs[:n_pre], refs[n_pre:]
        ins = refs[:n_in]
        srcs = refs[n_in:n_in + n_side]
        outs = refs[n_in + n_side:n_in + n_side + n_out]
        side_outs = refs[n_in + n_side + n_out:n_in + 2 * n_side + n_out]
        scratch = refs[n_in + 2 * n_side + n_out:]
        for src_ref, dst_ref in zip(srcs, side_outs):
            dst_ref[...] = src_ref[...].astype(BF16)
        body(*pre, *ins, *outs, *scratch)

    res = pl.pallas_call(
        wrapped,
        grid_spec=pltpu.PrefetchScalarGridSpec(
            num_scalar_prefetch=n_pre, grid=grid, in_specs=in_specs, out_specs=out_specs,
            scratch_shapes=scratch_shapes),
        out_shape=out_shapes, compiler_params=params, name=name)(*prefetch, *args)
    return res[:n_out], res[n_out:]


def _rmsnorm_body(x_ref, g_ref, o_ref):
    x = x_ref[...]
    ms = jnp.mean(x * x, axis=-1, keepdims=True)
    o_ref[...] = (x * lax.rsqrt(ms + EPS) * g_ref[...]).astype(o_ref.dtype)


def _rmsnorm(x2, g, out_dtype, tm=512):
    t, d = x2.shape
    return pl.pallas_call(
        _rmsnorm_body,
        grid=(t // tm,),
        in_specs=[pl.BlockSpec((tm, d), lambda i: (i, 0)),
                  pl.BlockSpec((1, d), lambda i: (0, 0))],
        out_specs=pl.BlockSpec((tm, d), lambda i: (i, 0)),
        out_shape=jax.ShapeDtypeStruct((t, d), out_dtype),
        compiler_params=_params(("arbitrary",), 32),
        name="rmsnorm",
    )(x2, g.reshape(1, d))


def _matmul_body(*refs, has_res):
    if has_res:
        x_ref, w_ref, r_ref, o_ref, wb_ref = refs
    else:
        x_ref, w_ref, o_ref, wb_ref = refs

    @pl.when(pl.program_id(1) == 0)
    def _():
        wb_ref[...] = w_ref[...].astype(BF16)

    acc = jnp.dot(x_ref[...], wb_ref[...], preferred_element_type=F32)
    if has_res:
        acc = acc + r_ref[...]
    o_ref[...] = acc.astype(o_ref.dtype)


def _matmul(x, w, res, out_dtype, tm, tn, name, sides=()):
    m, k = x.shape
    n = w.shape[1]
    in_specs = [pl.BlockSpec((tm, k), lambda j, i: (i, 0)),
                pl.BlockSpec((k, tn), lambda j, i: (0, j))]
    args = [x, w]
    if res is not None:
        in_specs.append(pl.BlockSpec((tm, tn), lambda j, i: (i, j)))
        args.append(res)
    (out,), side_outs = _call_with_sides(
        functools.partial(_matmul_body, has_res=res is not None),
        grid=(n // tn, m // tm),
        in_specs=in_specs, args=args,
        out_specs=[pl.BlockSpec((tm, tn), lambda j, i: (i, j))],
        out_shapes=[jax.ShapeDtypeStruct((m, n), out_dtype)],
        scratch_shapes=[pltpu.VMEM((k, tn), BF16)],
        sides=sides,
        params=_params(("arbitrary", "arbitrary"), 52),
        name=name)
    return out, side_outs


def _retention_body(q_ref, k_ref, v_ref, g_ref, cos_ref, sin_ref, o_ref, state_ref,
                    *, block_rows, seq):
    L = block_rows
    half = RET_QK_DIM // 2
    head = pl.program_id(1)

    def log_gamma(rows):
        headf = (jnp.zeros((rows, 1), jnp.int32) + head).astype(F32)
        return jnp.log(1.0 - jnp.exp2(-5.0 - headf))

    log_g = log_gamma(L)
    n_i = lax.broadcasted_iota(jnp.int32, (L, L), 0)
    m_i = lax.broadcasted_iota(jnp.int32, (L, L), 1)
    dist = jnp.abs(n_i - m_i).astype(F32)
    visible = (m_i >> 6) <= (n_i >> 6)
    decay = jnp.where(visible, jnp.exp(log_g * dist), 0.0)
    r = lax.broadcasted_iota(jnp.int32, (L, 1), 0).astype(F32)
    q_decay = jnp.exp(log_g * (r + 1.0))
    k_decay = jnp.exp(log_g * (L - 1.0 - r))
    block_decay = jnp.exp(log_gamma(RET_QK_DIM) * float(L))
    scale = RET_QK_DIM ** -0.5

    state_ref[...] = jnp.zeros_like(state_ref)

    def body(s, carry):
        r0 = pl.multiple_of(s * L, L)
        cos = cos_ref[pl.ds(r0, L), :]
        sin = sin_ref[pl.ds(r0, L), :]

        def rot(t):
            t1 = t[:, :half]
            t2 = t[:, half:]
            return jnp.concatenate([t1 * cos - t2 * sin, t1 * sin + t2 * cos], axis=1)

        q = rot(q_ref[pl.ds(r0, L), :].astype(F32)).astype(BF16)
        kf = rot(k_ref[pl.ds(r0, L), :].astype(F32)) * scale
        v = v_ref[pl.ds(r0, L), :]
        s_qk = lax.dot_general(q, kf.astype(BF16), (((1,), (1,)), ((), ())),
                               preferred_element_type=F32) * decay
        inner = jnp.dot(s_qk.astype(BF16), v, preferred_element_type=F32)
        st = state_ref[...]
        cross = jnp.dot(q, st.astype(BF16), preferred_element_type=F32) * q_decay
        kd_t = (kf * k_decay).T.astype(BF16)
        state_ref[...] = st * block_decay + jnp.dot(kd_t, v, preferred_element_type=F32)
        o = inner + cross
        o = o * lax.rsqrt(jnp.mean(o * o, axis=-1, keepdims=True) + EPS)
        gate = g_ref[pl.ds(r0, L), :].astype(F32)
        o_ref[pl.ds(r0, L), :] = (_silu(gate) * o).astype(o_ref.dtype)
        return carry

    lax.fori_loop(0, seq // L, body, 0, unroll=2)


def _retention(proj, cos, sin, sides=(), block_rows=256):
    b, s, _ = proj.shape
    nq = D_MODEL // RET_QK_DIM
    nv = RET_HV // RET_V_DIM
    (out,), side_outs = _call_with_sides(
        functools.partial(_retention_body, block_rows=block_rows, seq=s),
        grid=(b, RET_HEADS),
        in_specs=[
            pl.BlockSpec((None, s, RET_QK_DIM), lambda i, h: (i, 0, h)),
            pl.BlockSpec((None, s, RET_QK_DIM), lambda i, h: (i, 0, nq + h)),
            pl.BlockSpec((None, s, RET_V_DIM), lambda i, h: (i, 0, nv + h)),
            pl.BlockSpec((None, s, RET_V_DIM), lambda i, h: (i, 0, 2 * nv + h)),
            pl.BlockSpec((s, RET_QK_DIM // 2), lambda i, h: (0, 0)),
            pl.BlockSpec((s, RET_QK_DIM // 2), lambda i, h: (0, 0)),
        ],
        args=[proj, proj, proj, proj, cos, sin],
        out_specs=[pl.BlockSpec((None, s, RET_V_DIM), lambda i, h: (i, 0, h))],
        out_shapes=[jax.ShapeDtypeStruct((b, s, RET_HV), BF16)],
        scratch_shapes=[pltpu.VMEM((RET_QK_DIM, RET_V_DIM), F32)],
        sides=sides,
        params=_params(("arbitrary", "arbitrary"), 48),
        name="retention_core")
    return out, side_outs


ATT_PAD = LEFT_CHUNKS * CHUNK
ATT_LQ = 256
ATT_WIN = ATT_LQ + ATT_PAD
ATT_ROLL_W = ATT_LQ + ATT_WIN
REL_PAD = 384


def _attention_body(q_ref, k_ref, v_ref, rb_ref, o_ref, bm_ref, *, seq):
    @pl.when(pl.program_id(1) == 0)
    def _():
        t = lax.broadcasted_iota(jnp.int32, (REL_PAD, ATT_ROLL_W), 1)
        j = lax.broadcasted_iota(jnp.int32, (REL_PAD, ATT_ROLL_W), 0)
        idx = jnp.where(t <= ATT_WIN, jnp.clip(ATT_PAD - t, -REL_CLIP, REL_CLIP),
                        REL_CLIP) + REL_CLIP
        onehot = jnp.where(idx == j, 1.0, 0.0).astype(BF16)
        rb = rb_ref[...]
        hi = rb.astype(BF16)
        rest = rb - hi.astype(F32)
        mid = rest.astype(BF16)
        lo = (rest - mid.astype(F32)).astype(BF16)
        g = (jnp.dot(hi, onehot, preferred_element_type=F32)
             + jnp.dot(mid, onehot, preferred_element_type=F32)
             + jnp.dot(lo, onehot, preferred_element_type=F32))
        rows = jnp.broadcast_to(g[0:1, :], (ATT_LQ, ATT_ROLL_W))
        row = lax.broadcasted_iota(jnp.int32, (ATT_LQ, ATT_ROLL_W), 0)
        shift = 1
        while shift < ATT_LQ:
            rows = jnp.where((row & shift) != 0, pltpu.roll(rows, shift, 1), rows)
            shift *= 2
        qc = lax.broadcasted_iota(jnp.int32, (ATT_LQ, ATT_WIN), 0) >> 6
        kc = lax.broadcasted_iota(jnp.int32, (ATT_LQ, ATT_WIN), 1) >> 6
        ahead = kc - qc
        banded = jnp.where(ahead <= LEFT_CHUNKS, rows[:, :ATT_WIN], NEG_INF)
        bm_ref[...] = jnp.where(ahead >= 0, banded, NEG_INF)

    scale = ATT_HEAD_DIM ** -0.5
    for q0 in range(0, seq, ATT_LQ):
        k_lo = max(0, q0 - ATT_PAD)
        width = q0 + ATT_LQ - k_lo
        q = q_ref[q0:q0 + ATT_LQ, :]
        kw = k_ref[k_lo:k_lo + width, :]
        vw = v_ref[k_lo:k_lo + width, :]
        logits = lax.dot_general(q, kw, (((1,), (1,)), ((), ())),
                                 preferred_element_type=F32) * scale
        logits = logits + bm_ref[:, ATT_WIN - width:]
        mx = jnp.max(logits, axis=-1, keepdims=True)
        p = jnp.exp(logits - mx)
        denom = jnp.sum(p, axis=-1, keepdims=True)
        o = jnp.dot(p.astype(BF16), vw, preferred_element_type=F32) / denom
        o_ref[q0:q0 + ATT_LQ, :] = o.astype(o_ref.dtype)


def _attention(qkv, rel_bias, sides=()):
    b, s, _ = qkv.shape
    rb = jnp.zeros((ATT_HEADS, 8, REL_PAD), F32).at[:, 0, :2 * REL_CLIP + 1].set(rel_bias)
    (out,), side_outs = _call_with_sides(
        functools.partial(_attention_body, seq=s),
        grid=(ATT_HEADS, b),
        in_specs=[
            pl.BlockSpec((None, s, ATT_HEAD_DIM), lambda h, i: (i, 0, h)),
            pl.BlockSpec((None, s, ATT_HEAD_DIM), lambda h, i: (i, 0, ATT_HEADS + h)),
            pl.BlockSpec((None, s, ATT_HEAD_DIM), lambda h, i: (i, 0, 2 * ATT_HEADS + h)),
            pl.BlockSpec((None, 8, REL_PAD), lambda h, i: (h, 0, 0)),
        ],
        args=[qkv, qkv, qkv, rb],
        out_specs=[pl.BlockSpec((None, s, ATT_HEAD_DIM), lambda h, i: (i, 0, h))],
        out_shapes=[jax.ShapeDtypeStruct((b, s, D_MODEL), BF16)],
        scratch_shapes=[pltpu.VMEM((ATT_LQ, ATT_WIN), F32)],
        sides=sides,
        params=_params(("arbitrary", "arbitrary"), 48),
        name="chunk_attention")
    return out, side_outs


def _ffn_body(te_ref, nu_ref, tr_ref, *refs, normalize, half_tiles):
    del te_ref, nu_ref
    if normalize:
        x_ref, g_ref, wg_ref, wu_ref, wo_ref, o_ref, xb_ref = refs
    else:
        x_ref, wg_ref, wu_ref, wo_ref, o_ref, xb_ref = refs
    tm = xb_ref.shape[0]

    @pl.when(pl.program_id(1) == 0)
    def _():
        x = x_ref[...]
        if normalize:
            ms = jnp.mean(x * x, axis=-1, keepdims=True)
            xb_ref[...] = (x * lax.rsqrt(ms + EPS) * g_ref[...]).astype(BF16)
            o_ref[...] = x
        else:
            xb_ref[...] = x.astype(BF16)
            o_ref[...] = jnp.zeros_like(o_ref)

    def accumulate(m):
        xb = xb_ref[:m, :]
        gate = jnp.dot(xb, wg_ref[...], preferred_element_type=F32)
        up = jnp.dot(xb, wu_ref[...], preferred_element_type=F32)
        act = (_silu(gate) * up).astype(BF16)
        o_ref[:m, :] += jnp.dot(act, wo_ref[...], preferred_element_type=F32)

    rows_in_tile = tr_ref[pl.program_id(0)]
    if half_tiles:
        @pl.when(rows_in_tile > tm // 2)
        def _():
            accumulate(tm)

        @pl.when(jnp.logical_and(rows_in_tile > 0, rows_in_tile <= tm // 2))
        def _():
            accumulate(tm // 2)
    else:
        @pl.when(rows_in_tile > 0)
        def _():
            accumulate(tm)


def _ffn(x, gain, w_gate, w_up, w_out, tile_expert, n_used, tile_rows, tm, tf, vmem_mib, name,
         half_tiles=False, sides=()):
    rows, d = x.shape
    f = w_out.shape[1]
    nj = f // tf
    nt = rows // tm
    normalize = gain is not None

    def col(i, j, nu):
        return jnp.where(i < nu[0], j, nj - 1)

    in_specs = [pl.BlockSpec(
        (tm, d), lambda i, j, te, nu, tr: (jnp.maximum(jnp.minimum(i, nu[0] - 1), 0), 0))]
    args = [x]
    if normalize:
        in_specs.append(pl.BlockSpec((1, d), lambda i, j, te, nu, tr: (0, 0)))
        args.append(gain.reshape(1, d))
    in_specs += [
        pl.BlockSpec((None, d, tf), lambda i, j, te, nu, tr: (te[i], 0, col(i, j, nu))),
        pl.BlockSpec((None, d, tf), lambda i, j, te, nu, tr: (te[i], 0, col(i, j, nu))),
        pl.BlockSpec((None, tf, d), lambda i, j, te, nu, tr: (te[i], col(i, j, nu), 0)),
    ]
    args += [w_gate, w_up, w_out]
    (out,), side_outs = _call_with_sides(
        functools.partial(_ffn_body, normalize=normalize, half_tiles=half_tiles),
        grid=(nt, nj),
        in_specs=in_specs, args=args,
        out_specs=[pl.BlockSpec((tm, d), lambda i, j, te, nu, tr: (i, 0))],
        out_shapes=[jax.ShapeDtypeStruct((rows, d), F32)],
        scratch_shapes=[pltpu.VMEM((tm, d), BF16)],
        sides=sides,
        params=_params(("arbitrary", "arbitrary"), vmem_mib),
        name=name,
        prefetch=(tile_expert, n_used, tile_rows))
    return out, side_outs


def _normed(x_ref, g_ref):
    x = x_ref[...]
    ms = jnp.mean(x * x, axis=-1, keepdims=True)
    return x * lax.rsqrt(ms + EPS) * g_ref[...]


def _router_body(x_ref, g_ref, wr_ref, o_ref, cnt_ref, carry_ref, *, tm):
    @pl.when(pl.program_id(0) == 0)
    def _():
        carry_ref[...] = jnp.zeros_like(carry_ref)

    logits = jnp.dot(_normed(x_ref, g_ref), wr_ref[...], preferred_element_type=F32,
                     precision=lax.Precision.HIGHEST)
    lt = logits.T[:N_EXPERTS, :]
    e = lax.broadcasted_iota(jnp.int32, (N_EXPERTS, tm), 0)
    m1 = jnp.max(lt, axis=0, keepdims=True)
    i1 = jnp.min(jnp.where(lt == m1, e, N_EXPERTS), axis=0, keepdims=True)
    lt2 = jnp.where(e == i1, -jnp.inf, lt)
    m2 = jnp.max(lt2, axis=0, keepdims=True)
    i2 = jnp.min(jnp.where(lt2 == m2, e, N_EXPERTS), axis=0, keepdims=True)
    ex = jnp.exp(m2 - m1)
    g1 = 1.0 / (1.0 + ex)
    g2 = ex / (1.0 + ex)

    oh1 = jnp.where(e == i1, 1.0, 0.0)
    oh2 = jnp.where(e == i2, 1.0, 0.0)
    before = jnp.where(lax.broadcasted_iota(jnp.int32, (tm, tm), 0)
                       < lax.broadcasted_iota(jnp.int32, (tm, tm), 1), 1.0, 0.0).astype(BF16)
    c1 = jnp.dot(oh1.astype(BF16), before, preferred_element_type=F32)
    c2 = jnp.dot(oh2.astype(BF16), before, preferred_element_type=F32)
    t1 = jnp.sum(oh1, axis=1, keepdims=True)
    t2 = jnp.sum(oh2, axis=1, keepdims=True)
    carry = carry_ref[...]
    base = carry[:, :1]
    r1 = jnp.sum(oh1 * (base + c1), axis=0, keepdims=True)
    r2 = jnp.sum(oh2 * (base + t1 + c2), axis=0, keepdims=True)
    carry = carry + t1 + t2
    carry_ref[...] = carry
    cnt_ref[...] = carry

    out = jnp.where(e == 0, i1.astype(F32),
          jnp.where(e == 1, i2.astype(F32),
          jnp.where(e == 2, g1,
          jnp.where(e == 3, g2,
          jnp.where(e == 4, r1,
          jnp.where(e == 5, r2, 0.0))))))
    o_ref[...] = out


def _router(x2, gain, w_router, tm=512):
    t, d = x2.shape
    wr = jnp.pad(w_router, ((0, 0), (0, LANES - N_EXPERTS)))
    return pl.pallas_call(
        functools.partial(_router_body, tm=tm),
        grid=(t // tm,),
        in_specs=[pl.BlockSpec((tm, d), lambda i: (i, 0)),
                  pl.BlockSpec((1, d), lambda i: (0, 0)),
                  pl.BlockSpec((d, LANES), lambda i: (0, 0))],
        out_specs=[pl.BlockSpec((N_EXPERTS, tm), lambda i: (0, i)),
                   pl.BlockSpec((N_EXPERTS, LANES), lambda i: (0, 0))],
        out_shape=[jax.ShapeDtypeStruct((N_EXPERTS, t), F32),
                   jax.ShapeDtypeStruct((N_EXPERTS, LANES), F32)],
        scratch_shapes=[pltpu.VMEM((N_EXPERTS, LANES), F32)],
        compiler_params=_params(("arbitrary",), 32),
        name="router_top2",
    )(x2, gain.reshape(1, d), wr)


def _dispatch_body(pos_ref, fill_ref, x_ref, g_ref, xs_ref, hbuf, zbuf, sem, zsem,
                   *, tk, tokens, n_tiles, n_steps):
    i = pl.program_id(0)
    slot = i % 2
    base = i * tk

    def tile_fill(tile):
        return pltpu.make_async_copy(zbuf, xs_ref.at[pl.ds(tile * MOE_TM, MOE_TM)], zsem)

    @pl.when(i == 0)
    def _():
        zbuf[...] = jnp.zeros_like(zbuf)

        def start_fill(tile, carry):
            @pl.when(fill_ref[tile] != 0)
            def _():
                tile_fill(tile).start()
            return carry

        lax.fori_loop(0, n_tiles, start_fill, 0)

        def wait_fill(tile, carry):
            @pl.when(fill_ref[tile] != 0)
            def _():
                tile_fill(tile).wait()
            return carry

        lax.fori_loop(0, n_tiles, wait_fill, 0)

    def wait_slot(s):
        for _ in range(2):
            pltpu.make_async_copy(hbuf.at[s], xs_ref.at[pl.ds(0, tk)], sem.at[s]).wait()

    @pl.when(i >= 2)
    def _():
        wait_slot(slot)

    hbuf[slot] = _normed(x_ref, g_ref)

    def start(group, carry):
        r0 = pl.multiple_of(group * 8, 8)
        for u in range(8):
            for assignment in range(2):
                dst = pos_ref[assignment * tokens + base + r0 + u]
                pltpu.make_async_copy(hbuf.at[slot, pl.ds(r0 + u, 1)], xs_ref.at[pl.ds(dst, 1)],
                                      sem.at[slot]).start()
        return carry

    lax.fori_loop(0, tk // 8, start, 0)

    @pl.when(i == n_steps - 1)
    def _():
        if n_steps >= 2:
            wait_slot(1 - slot)
        wait_slot(slot)


def _dispatch(x2, gain, pos_flat, fill_flag, n_tiles, tk=256):
    t, d = x2.shape
    return pl.pallas_call(
        functools.partial(_dispatch_body, tk=tk, tokens=t, n_tiles=n_tiles, n_steps=t // tk),
        grid_spec=pltpu.PrefetchScalarGridSpec(
            num_scalar_prefetch=2,
            grid=(t // tk,),
            in_specs=[pl.BlockSpec((tk, d), lambda i, pos, fill: (i, 0)),
                      pl.BlockSpec((1, d), lambda i, pos, fill: (0, 0))],
            out_specs=pl.BlockSpec(memory_space=pl.ANY),
            scratch_shapes=[pltpu.VMEM((2, tk, d), F32),
                            pltpu.VMEM((MOE_TM, d), F32),
                            pltpu.SemaphoreType.DMA((2,)),
                            pltpu.SemaphoreType.DMA(())]),
        out_shape=jax.ShapeDtypeStruct((n_tiles * MOE_TM, d), F32),
        compiler_params=_params(("arbitrary",), 32),
        name="moe_dispatch",
    )(pos_flat, fill_flag, x2, gain.reshape(1, d))


def _combine_body(pos_ref, x_ref, gates_ref, gf_ref, y_ref, o_ref, ybuf, sem,
                  *, tk, tokens, n_steps):
    i = pl.program_id(0)
    slot = i % 2

    def gather(step, s):
        base = step * tk

        def start(r, carry):
            for assignment in range(2):
                src = pos_ref[assignment * tokens + base + r]
                pltpu.make_async_copy(y_ref.at[pl.ds(src, 1)],
                                      ybuf.at[s, assignment, pl.ds(r, 1)], sem.at[s]).start()
            return carry

        lax.fori_loop(0, tk, start, 0, unroll=8)

    @pl.when(i == 0)
    def _():
        gather(0, 0)

    @pl.when(i + 1 < n_steps)
    def _():
        gather(i + 1, 1 - slot)

    for assignment in range(2):
        pltpu.make_async_copy(y_ref.at[pl.ds(0, tk)], ybuf.at[slot, assignment],
                              sem.at[slot]).wait()

    gates = gates_ref[...]
    x = x_ref[...] + gates[:, 0:1] * ybuf[slot, 0] + gates[:, 1:2] * ybuf[slot, 1]
    ms = jnp.mean(x * x, axis=-1, keepdims=True)
    o_ref[...] = x * lax.rsqrt(ms + EPS) * gf_ref[...]


def _combine(x2, y, pos_flat, gates, g_final, tk=256):
    t, d = x2.shape
    return pl.pallas_call(
        functools.partial(_combine_body, tk=tk, tokens=t, n_steps=t // tk),
        grid_spec=pltpu.PrefetchScalarGridSpec(
            num_scalar_prefetch=1,
            grid=(t // tk,),
            in_specs=[pl.BlockSpec((tk, d), lambda i, pos: (i, 0)),
                      pl.BlockSpec((tk, 2), lambda i, pos: (i, 0)),
                      pl.BlockSpec((1, d), lambda i, pos: (0, 0)),
                      pl.BlockSpec(memory_space=pl.ANY)],
            out_specs=pl.BlockSpec((tk, d), lambda i, pos: (i, 0)),
            scratch_shapes=[pltpu.VMEM((2, 2, tk, d), F32),
                            pltpu.SemaphoreType.DMA((2,))]),
        out_shape=jax.ShapeDtypeStruct((t, d), F32),
        compiler_params=_params(("arbitrary",), 32),
        name="moe_combine",
    )(pos_flat, x2, gates, g_final.reshape(1, d), y)


MOE_TM = 512
DENSE_TF = 512
EXPERT_TF = 1024
PROJ_TM = 1024
PROJ_TN = 1024


def _routing_tables(route, counts, n_tiles):
    idx1 = route[0].astype(jnp.int32)
    idx2 = route[1].astype(jnp.int32)
    cnt = counts[:, 0].astype(jnp.int32)
    tiles_e = (cnt + MOE_TM - 1) // MOE_TM
    tile_end = jnp.cumsum(tiles_e)
    offset = (tile_end - tiles_e) * MOE_TM
    pos1 = offset[idx1] + route[4].astype(jnp.int32)
    pos2 = offset[idx2] + route[5].astype(jnp.int32)
    n_used = tile_end[-1]
    tiles = jnp.arange(n_tiles, dtype=jnp.int32)
    te = jnp.sum((tiles[:, None] >= tile_end[None, :]).astype(jnp.int32), axis=1)
    te = jnp.minimum(te, N_EXPERTS - 1)
    te = jnp.where(tiles < n_used, te, te[n_used - 1])
    last_of_expert = jnp.any((tiles[:, None] == tile_end[None, :] - 1) & (tiles_e[None, :] > 0),
                             axis=1)
    fill_flag = (last_of_expert | (tiles >= n_used)).astype(jnp.int32)
    live = cnt[te] - (tiles - (tile_end - tiles_e)[te]) * MOE_TM
    tile_rows = jnp.where(tiles < n_used, jnp.clip(live, 0, MOE_TM), 0).astype(jnp.int32)
    pos_flat = jnp.concatenate([pos1, pos2]).astype(jnp.int32)
    gates = jnp.stack([route[2], route[3]], axis=1)
    return pos_flat, gates, te, n_used.reshape(1).astype(jnp.int32), fill_flag, tile_rows


def kernel(x, ln_mix, ln_ffn, ln_final, ret_w_in, ret_w_out, ffn_w_in, ffn_w_out,
           att_w_in, att_w_out, att_rel_bias, moe_router, moe_w_in, moe_w_out):
    b, s, d = x.shape
    t = b * s
    x2 = x.reshape(t, d)
    f_dense = ffn_w_out.shape[1]
    f_expert = moe_w_out.shape[2]

    moe_in2 = moe_w_in[0].reshape(N_EXPERTS * d, 2 * f_expert)
    moe_out2 = moe_w_out[0].reshape(N_EXPERTS * f_expert, d)
    ret_steps = b * RET_HEADS
    att_steps = ATT_HEADS * b
    cast_moe_gate = _SideCast(moe_in2, 128, f_expert, 0)
    cast_moe_up = _SideCast(moe_in2, att_steps, f_expert, 1)
    cast_moe_out = _SideCast(moe_out2, 256, d)
    cast_ffn_gate = _SideCast(ffn_w_in[0], ret_steps, f_dense, 0)
    cast_ffn_up = _SideCast(ffn_w_in[0], ret_steps, f_dense, 1)
    cast_ffn_out = _SideCast(ffn_w_out[0], ret_steps // 2, d)

    h = _rmsnorm(x2, ln_mix[0], BF16)
    proj, (moe_gate_bf,) = _matmul(h, ret_w_in[0], None, BF16, PROJ_TM, PROJ_TN, "ret_proj_in",
                                   sides=[cast_moe_gate])
    pos = jnp.arange(s, dtype=F32)
    inv_freq = ROPE_BASE ** (-jnp.arange(0, RET_QK_DIM, 2, dtype=F32) / RET_QK_DIM)
    ang = pos[:, None] * inv_freq[None, :]
    y, (ffn_gate_bf, ffn_up_bf, ffn_out_bf) = _retention(
        proj.reshape(b, s, -1), jnp.cos(ang), jnp.sin(ang),
        sides=[cast_ffn_gate, cast_ffn_up, cast_ffn_out])
    x2, _ = _matmul(y.reshape(t, RET_HV), ret_w_out[0], x2, F32, PROJ_TM, PROJ_TN // 2,
                    "ret_proj_out")

    dense_tiles = t // MOE_TM
    x2, (moe_out_bf,) = _ffn(
        x2, ln_ffn[0], ffn_gate_bf[None], ffn_up_bf[None], ffn_out_bf[None],
        jnp.zeros((dense_tiles,), jnp.int32), jnp.full((1,), dense_tiles, jnp.int32),
        jnp.full((dense_tiles,), MOE_TM, jnp.int32),
        MOE_TM, DENSE_TF, 48, "dense_swiglu", sides=[cast_moe_out])

    h = _rmsnorm(x2, ln_mix[1], BF16)
    qkv, _ = _matmul(h, att_w_in[0], None, BF16, PROJ_TM, PROJ_TN, "att_proj_in")
    a, (moe_up_bf,) = _attention(qkv.reshape(b, s, -1), att_rel_bias[0], sides=[cast_moe_up])
    x2, _ = _matmul(a.reshape(t, d), att_w_out[0], x2, F32, PROJ_TM, PROJ_TN, "att_proj_out")

    route, counts = _router(x2, ln_ffn[1], moe_router[0])
    n_tiles = 2 * t // MOE_TM + N_EXPERTS
    pos_flat, gates, te, n_used, fill_flag, tile_rows = _routing_tables(route, counts, n_tiles)
    xs = _dispatch(x2, ln_ffn[1], pos_flat, fill_flag, n_tiles)
    ys, _ = _ffn(xs, None, moe_gate_bf.reshape(N_EXPERTS, d, f_expert),
                 moe_up_bf.reshape(N_EXPERTS, d, f_expert),
                 moe_out_bf.reshape(N_EXPERTS, f_expert, d), te, n_used, tile_rows,
                 MOE_TM, EXPERT_TF, 56, "expert_swiglu", half_tiles=True)
    out = _combine(x2, ys, pos_flat, gates, ln_final)
    return out.reshape(b, s, d)
```

```python
import functools
import math
from typing import NamedTuple

import jax
import jax.numpy as jnp
from jax import lax
from jax.experimental import pallas as pl
from jax.experimental.pallas import tpu as pltpu

D_MODEL = 2048
CHUNK = 64
EPS = 1e-6
RET_HEADS = 8
RET_QK_DIM = D_MODEL // RET_HEADS
RET_V_DIM = 2 * RET_QK_DIM
RET_HV = RET_HEADS * RET_V_DIM
ROPE_BASE = 10000.0
ATT_HEADS = 16
ATT_HEAD_DIM = D_MODEL // ATT_HEADS
LEFT_CHUNKS = 8
REL_CLIP = 128
NEG_INF = -1e30
N_EXPERTS = 8

LANES = 128
MIB = 1024 * 1024

F32 = jnp.float32
BF16 = jnp.bfloat16


def _params(semantics, vmem_mib):
    return pltpu.CompilerParams(
        dimension_semantics=semantics, vmem_limit_bytes=vmem_mib * MIB)


def _silu(x):
    return x * (1.0 / (1.0 + jnp.exp(-x)))


class _SideCast(NamedTuple):
    src: jax.Array
    num_blocks: int
    block_cols: int
    col_block: int = 0


def _call_with_sides(body, *, grid, in_specs, args, out_specs, out_shapes, scratch_shapes,
                     sides, params, name, prefetch=()):
    steps = math.prod(grid)
    n_pre, n_in, n_out, n_side = len(prefetch), len(args), len(out_shapes), len(sides)
    in_specs, args = list(in_specs), list(args)
    out_specs, out_shapes = list(out_specs), list(out_shapes)

    def block_spec(side, col_block):
        def index(*ids):
            step = ids[0]
            for extent, pid in zip(grid[1:], ids[1:len(grid)]):
                step = step * extent + pid
            return ((step * side.num_blocks) // steps, col_block)
        return pl.BlockSpec((side.src.shape[0] // side.num_blocks, side.block_cols), index)

    for side in sides:
        assert side.num_blocks <= steps and side.src.shape[0] % side.num_blocks == 0
        in_specs.append(block_spec(side, side.col_block))
        args.append(side.src)
        out_specs.append(block_spec(side, 0))
        out_shapes.append(jax.ShapeDtypeStruct((side.src.shape[0], side.block_cols), BF16))

    def wrapped(*refs):
        pre, refs = refs[:n_pre], refs[n_pre:]
        ins = refs[:n_in]
        srcs = refs[n_in:n_in + n_side]
        outs = refs[n_in + n_side:n_in + n_side + n_out]
        side_outs = refs[n_in + n_side + n_out:n_in + 2 * n_side + n_out]
        scratch = refs[n_in + 2 * n_side + n_out:]
        for src_ref, dst_ref in zip(srcs, side_outs):
            dst_ref[...] = src_ref[...].astype(BF16)
        body(*pre, *ins, *outs, *scratch)

    res = pl.pallas_call(
        wrapped,
        grid_spec=pltpu.PrefetchScalarGridSpec(
            num_scalar_prefetch=n_pre, grid=grid, in_specs=in_specs, out_specs=out_specs,
            scratch_shapes=scratch_shapes),
        out_shape=out_shapes, compiler_params=params, name=name)(*prefetch, *args)
    return res[:n_out], res[n_out:]


def _rmsnorm_body(x_ref, g_ref, o_ref):
    x = x_ref[...]
    ms = jnp.mean(x * x, axis=-1, keepdims=True)
    o_ref[...] = (x * lax.rsqrt(ms + EPS) * g_ref[...]).astype(o_ref.dtype)


def _rmsnorm(x2, g, out_dtype, tm=512):
    t, d = x2.shape
    return pl.pallas_call(
        _rmsnorm_body,
        grid=(t // tm,),
        in_specs=[pl.BlockSpec((tm, d), lambda i: (i, 0)),
                  pl.BlockSpec((1, d), lambda i: (0, 0))],
        out_specs=pl.BlockSpec((tm, d), lambda i: (i, 0)),
        out_shape=jax.ShapeDtypeStruct((t, d), out_dtype),
        compiler_params=_params(("arbitrary",), 32),
        name="rmsnorm",
    )(x2, g.reshape(1, d))


def _matmul_body(*refs, has_res):
    if has_res:
        x_ref, w_ref, r_ref, o_ref, wb_ref = refs
    else:
        x_ref, w_ref, o_ref, wb_ref = refs

    @pl.when(pl.program_id(1) == 0)
    def _():
        wb_ref[...] = w_ref[...].astype(BF16)

    acc = jnp.dot(x_ref[...], wb_ref[...], preferred_element_type=F32)
    if has_res:
        acc = acc + r_ref[...]
    o_ref[...] = acc.astype(o_ref.dtype)


def _matmul(x, w, res, out_dtype, tm, tn, name, sides=()):
    m, k = x.shape
    n = w.shape[1]
    in_specs = [pl.BlockSpec((tm, k), lambda j, i: (i, 0)),
                pl.BlockSpec((k, tn), lambda j, i: (0, j))]
    args = [x, w]
    if res is not None:
        in_specs.append(pl.BlockSpec((tm, tn), lambda j, i: (i, j)))
        args.append(res)
    (out,), side_outs = _call_with_sides(
        functools.partial(_matmul_body, has_res=res is not None),
        grid=(n // tn, m // tm),
        in_specs=in_specs, args=args,
        out_specs=[pl.BlockSpec((tm, tn), lambda j, i: (i, j))],
        out_shapes=[jax.ShapeDtypeStruct((m, n), out_dtype)],
        scratch_shapes=[pltpu.VMEM((k, tn), BF16)],
        sides=sides,
        params=_params(("arbitrary", "arbitrary"), 52),
        name=name)
    return out, side_outs


def _retention_body(q_ref, k_ref, v_ref, g_ref, cos_ref, sin_ref, o_ref, state_ref,
                    *, block_rows, seq):
    L = block_rows
    half = RET_QK_DIM // 2
    head = pl.program_id(1)

    def log_gamma(rows):
        headf = (jnp.zeros((rows, 1), jnp.int32) + head).astype(F32)
        return jnp.log(1.0 - jnp.exp2(-5.0 - headf))

    log_g = log_gamma(L)
    n_i = lax.broadcasted_iota(jnp.int32, (L, L), 0)
    m_i = lax.broadcasted_iota(jnp.int32, (L, L), 1)
    dist = jnp.abs(n_i - m_i).astype(F32)
    visible = (m_i >> 6) <= (n_i >> 6)
    decay = jnp.where(visible, jnp.exp(log_g * dist), 0.0)
    r = lax.broadcasted_iota(jnp.int32, (L, 1), 0).astype(F32)
    q_decay = jnp.exp(log_g * (r + 1.0))
    k_decay = jnp.exp(log_g * (L - 1.0 - r))
    block_decay = jnp.exp(log_gamma(RET_QK_DIM) * float(L))
    scale = RET_QK_DIM ** -0.5

    state_ref[...] = jnp.zeros_like(state_ref)

    def body(s, carry):
        r0 = pl.multiple_of(s * L, L)
        cos = cos_ref[pl.ds(r0, L), :]
        sin = sin_ref[pl.ds(r0, L), :]

        def rot(t):
            t1 = t[:, :half]
            t2 = t[:, half:]
            return jnp.concatenate([t1 * cos - t2 * sin, t1 * sin + t2 * cos], axis=1)

        q = rot(q_ref[pl.ds(r0, L), :].astype(F32)).astype(BF16)
        kf = rot(k_ref[pl.ds(r0, L), :].astype(F32)) * scale
        v = v_ref[pl.ds(r0, L), :]
        s_qk = lax.dot_general(q, kf.astype(BF16), (((1,), (1,)), ((), ())),
                               preferred_element_type=F32) * decay
        inner = jnp.dot(s_qk.astype(BF16), v, preferred_element_type=F32)
        st = state_ref[...]
        cross = jnp.dot(q, st.astype(BF16), preferred_element_type=F32) * q_decay
        kd_t = (kf * k_decay).T.astype(BF16)
        state_ref[...] = st * block_decay + jnp.dot(kd_t, v, preferred_element_type=F32)
        o = inner + cross
        o = o * lax.rsqrt(jnp.mean(o * o, axis=-1, keepdims=True) + EPS)
        gate = g_ref[pl.ds(r0, L), :].astype(F32)
        o_ref[pl.ds(r0, L), :] = (_silu(gate) * o).astype(o_ref.dtype)
        return carry

    lax.fori_loop(0, seq // L, body, 0, unroll=2)


def _retention(proj, cos, sin, sides=(), block_rows=256):
    b, s, _ = proj.shape
    nq = D_MODEL // RET_QK_DIM
    nv = RET_HV // RET_V_DIM
    (out,), side_outs = _call_with_sides(
        functools.partial(_retention_body, block_rows=block_rows, seq=s),
        grid=(b, RET_HEADS),
        in_specs=[
            pl.BlockSpec((None, s, RET_QK_DIM), lambda i, h: (i, 0, h)),
            pl.BlockSpec((None, s, RET_QK_DIM), lambda i, h: (i, 0, nq + h)),
            pl.BlockSpec((None, s, RET_V_DIM), lambda i, h: (i, 0, nv + h)),
            pl.BlockSpec((None, s, RET_V_DIM), lambda i, h: (i, 0, 2 * nv + h)),
            pl.BlockSpec((s, RET_QK_DIM // 2), lambda i, h: (0, 0)),
            pl.BlockSpec((s, RET_QK_DIM // 2), lambda i, h: (0, 0)),
        ],
        args=[proj, proj, proj, proj, cos, sin],
        out_specs=[pl.BlockSpec((None, s, RET_V_DIM), lambda i, h: (i, 0, h))],
        out_shapes=[jax.ShapeDtypeStruct((b, s, RET_HV), BF16)],
        scratch_shapes=[pltpu.VMEM((RET_QK_DIM, RET_V_DIM), F32)],
        sides=sides,
        params=_params(("arbitrary", "arbitrary"), 48),
        name="retention_core")
    return out, side_outs


ATT_PAD = LEFT_CHUNKS * CHUNK
ATT_LQ = 256
ATT_WIN = ATT_LQ + ATT_PAD
ATT_ROLL_W = ATT_LQ + ATT_WIN
REL_PAD = 384


def _attention_body(q_ref, k_ref, v_ref, rb_ref, o_ref, bm_ref, *, seq):
    @pl.when(pl.program_id(1) == 0)
    def _():
        t = lax.broadcasted_iota(jnp.int32, (REL_PAD, ATT_ROLL_W), 1)
        j = lax.broadcasted_iota(jnp.int32, (REL_PAD, ATT_ROLL_W), 0)
        idx = jnp.where(t <= ATT_WIN, jnp.clip(ATT_PAD - t, -REL_CLIP, REL_CLIP),
                        REL_CLIP) + REL_CLIP
        onehot = jnp.where(idx == j, 1.0, 0.0).astype(BF16)
        rb = rb_ref[...]
        hi = rb.astype(BF16)
        rest = rb - hi.astype(F32)
        mid = rest.astype(BF16)
        lo = (rest - mid.astype(F32)).astype(BF16)
        g = (jnp.dot(hi, onehot, preferred_element_type=F32)
             + jnp.dot(mid, onehot, preferred_element_type=F32)
             + jnp.dot(lo, onehot, preferred_element_type=F32))
        rows = jnp.broadcast_to(g[0:1, :], (ATT_LQ, ATT_ROLL_W))
        row = lax.broadcasted_iota(jnp.int32, (ATT_LQ, ATT_ROLL_W), 0)
        shift = 1
        while shift < ATT_LQ:
            rows = jnp.where((row & shift) != 0, pltpu.roll(rows, shift, 1), rows)
            shift *= 2
        qc = lax.broadcasted_iota(jnp.int32, (ATT_LQ, ATT_WIN), 0) >> 6
        kc = lax.broadcasted_iota(jnp.int32, (ATT_LQ, ATT_WIN), 1) >> 6
        ahead = kc - qc
        banded = jnp.where(ahead <= LEFT_CHUNKS, rows[:, :ATT_WIN], NEG_INF)
        bm_ref[...] = jnp.where(ahead >= 0, banded, NEG_INF)

    scale = ATT_HEAD_DIM ** -0.5
    for q0 in range(0, seq, ATT_LQ):
        k_lo = max(0, q0 - ATT_PAD)
        width = q0 + ATT_LQ - k_lo
        q = q_ref[q0:q0 + ATT_LQ, :]
        kw = k_ref[k_lo:k_lo + width, :]
        vw = v_ref[k_lo:k_lo + width, :]
        logits = lax.dot_general(q, kw, (((1,), (1,)), ((), ())),
                                 preferred_element_type=F32) * scale
        logits = logits + bm_ref[:, ATT_WIN - width:]
        mx = jnp.max(logits, axis=-1, keepdims=True)
        p = jnp.exp(logits - mx)
        denom = jnp.sum(p, axis=-1, keepdims=True)
        o = jnp.dot(p.astype(BF16), vw, preferred_element_type=F32) / denom
        o_ref[q0:q0 + ATT_LQ, :] = o.astype(o_ref.dtype)


def _attention(qkv, rel_bias, sides=()):
    b, s, _ = qkv.shape
    rb = jnp.zeros((ATT_HEADS, 8, REL_PAD), F32).at[:, 0, :2 * REL_CLIP + 1].set(rel_bias)
    (out,), side_outs = _call_with_sides(
        functools.partial(_attention_body, seq=s),
        grid=(ATT_HEADS, b),
        in_specs=[
            pl.BlockSpec((None, s, ATT_HEAD_DIM), lambda h, i: (i, 0, h)),
            pl.BlockSpec((None, s, ATT_HEAD_DIM), lambda h, i: (i, 0, ATT_HEADS + h)),
            pl.BlockSpec((None, s, ATT_HEAD_DIM), lambda h, i: (i, 0, 2 * ATT_HEADS + h)),
            pl.BlockSpec((None, 8, REL_PAD), lambda h, i: (h, 0, 0)),
        ],
        args=[qkv, qkv, qkv, rb],
        out_specs=[pl.BlockSpec((None, s, ATT_HEAD_DIM), lambda h, i: (i, 0, h))],
        out_shapes=[jax.ShapeDtypeStruct((b, s, D_MODEL), BF16)],
        scratch_shapes=[pltpu.VMEM((ATT_LQ, ATT_WIN), F32)],
        sides=sides,
        params=_params(("arbitrary", "arbitrary"), 48),
        name="chunk_attention")
    return out, side_outs


def _ffn_body(te_ref, nu_ref, tr_ref, *refs, normalize, half_tiles):
    del te_ref, nu_ref
    if normalize:
        x_ref, g_ref, g_next_ref, wg_ref, wu_ref, wo_ref, o_ref, h_next_ref, xb_ref = refs
    else:
        x_ref, wg_ref, wu_ref, wo_ref, o_ref, xb_ref = refs
    tm = xb_ref.shape[0]

    @pl.when(pl.program_id(1) == 0)
    def _():
        x = x_ref[...]
        if normalize:
            ms = jnp.mean(x * x, axis=-1, keepdims=True)
            xb_ref[...] = (x * lax.rsqrt(ms + EPS) * g_ref[...]).astype(BF16)
            o_ref[...] = x
        else:
            xb_ref[...] = x.astype(BF16)
            o_ref[...] = jnp.zeros_like(o_ref)

    def accumulate(m):
        xb = xb_ref[:m, :]
        gate = jnp.dot(xb, wg_ref[...], preferred_element_type=F32)
        up = jnp.dot(xb, wu_ref[...], preferred_element_type=F32)
        act = (_silu(gate) * up).astype(BF16)
        o_ref[:m, :] += jnp.dot(act, wo_ref[...], preferred_element_type=F32)

    rows_in_tile = tr_ref[pl.program_id(0)]
    if half_tiles:
        @pl.when(rows_in_tile > tm // 2)
        def _():
            accumulate(tm)

        @pl.when(jnp.logical_and(rows_in_tile > 0, rows_in_tile <= tm // 2))
        def _():
            accumulate(tm // 2)
    else:
        @pl.when(rows_in_tile > 0)
        def _():
            accumulate(tm)

    if normalize:
        @pl.when(pl.program_id(1) == pl.num_programs(1) - 1)
        def _():
            o = o_ref[...]
            ms = jnp.mean(o * o, axis=-1, keepdims=True)
            h_next_ref[...] = (o * lax.rsqrt(ms + EPS) * g_next_ref[...]).astype(BF16)


def _ffn(x, gains, w_gate, w_up, w_out, tile_expert, n_used, tile_rows, tm, tf, vmem_mib, name,
         half_tiles=False, sides=()):
    rows, d = x.shape
    f = w_out.shape[1]
    nj = f // tf
    nt = rows // tm
    normalize = gains is not None

    def col(i, j, nu):
        return jnp.where(i < nu[0], j, nj - 1)

    def row_block():
        return pl.BlockSpec((tm, d), lambda i, j, te, nu, tr: (i, 0))

    in_specs = [pl.BlockSpec(
        (tm, d), lambda i, j, te, nu, tr: (jnp.maximum(jnp.minimum(i, nu[0] - 1), 0), 0))]
    args = [x]
    out_specs = [row_block()]
    out_shapes = [jax.ShapeDtypeStruct((rows, d), F32)]
    if normalize:
        for gain in gains:
            in_specs.append(pl.BlockSpec((1, d), lambda i, j, te, nu, tr: (0, 0)))
            args.append(gain.reshape(1, d))
        out_specs.append(row_block())
        out_shapes.append(jax.ShapeDtypeStruct((rows, d), BF16))
    in_specs += [
        pl.BlockSpec((None, d, tf), lambda i, j, te, nu, tr: (te[i], 0, col(i, j, nu))),
        pl.BlockSpec((None, d, tf), lambda i, j, te, nu, tr: (te[i], 0, col(i, j, nu))),
        pl.BlockSpec((None, tf, d), lambda i, j, te, nu, tr: (te[i], col(i, j, nu), 0)),
    ]
    args += [w_gate, w_up, w_out]
    return _call_with_sides(
        functools.partial(_ffn_body, normalize=normalize, half_tiles=half_tiles),
        grid=(nt, nj),
        in_specs=in_specs, args=args,
        out_specs=out_specs, out_shapes=out_shapes,
        scratch_shapes=[pltpu.VMEM((tm, d), BF16)],
        sides=sides,
        params=_params(("arbitrary", "arbitrary"), vmem_mib),
        name=name,
        prefetch=(tile_expert, n_used, tile_rows))


def _normed(x_ref, g_ref):
    x = x_ref[...]
    ms = jnp.mean(x * x, axis=-1, keepdims=True)
    return x * lax.rsqrt(ms + EPS) * g_ref[...]


def _router_body(x_ref, g_ref, wr_ref, o_ref, cnt_ref, carry_ref, *, tm):
    @pl.when(pl.program_id(0) == 0)
    def _():
        carry_ref[...] = jnp.zeros_like(carry_ref)

    def split(a):
        hi = a.astype(BF16)
        return hi, (a - hi.astype(F32)).astype(BF16)

    h_hi, h_lo = split(_normed(x_ref, g_ref))
    w_hi, w_lo = split(wr_ref[...])
    logits = (jnp.dot(h_hi, w_hi, preferred_element_type=F32)
              + jnp.dot(h_hi, w_lo, preferred_element_type=F32)
              + jnp.dot(h_lo, w_hi, preferred_element_type=F32))
    lt = logits.T[:N_EXPERTS, :]
    e = lax.broadcasted_iota(jnp.int32, (N_EXPERTS, tm), 0)
    m1 = jnp.max(lt, axis=0, keepdims=True)
    i1 = jnp.min(jnp.where(lt == m1, e, N_EXPERTS), axis=0, keepdims=True)
    lt2 = jnp.where(e == i1, -jnp.inf, lt)
    m2 = jnp.max(lt2, axis=0, keepdims=True)
    i2 = jnp.min(jnp.where(lt2 == m2, e, N_EXPERTS), axis=0, keepdims=True)
    ex = jnp.exp(m2 - m1)
    g1 = 1.0 / (1.0 + ex)
    g2 = ex / (1.0 + ex)

    oh1 = jnp.where(e == i1, 1.0, 0.0)
    oh2 = jnp.where(e == i2, 1.0, 0.0)
    before = jnp.where(lax.broadcasted_iota(jnp.int32, (tm, tm), 0)
                       < lax.broadcasted_iota(jnp.int32, (tm, tm), 1), 1.0, 0.0).astype(BF16)
    c1 = jnp.dot(oh1.astype(BF16), before, preferred_element_type=F32)
    c2 = jnp.dot(oh2.astype(BF16), before, preferred_element_type=F32)
    t1 = jnp.sum(oh1, axis=1, keepdims=True)
    t2 = jnp.sum(oh2, axis=1, keepdims=True)
    carry = carry_ref[...]
    base = carry[:, :1]
    r1 = jnp.sum(oh1 * (base + c1), axis=0, keepdims=True)
    r2 = jnp.sum(oh2 * (base + t1 + c2), axis=0, keepdims=True)
    carry = carry + t1 + t2
    carry_ref[...] = carry
    cnt_ref[...] = carry

    out = jnp.where(e == 0, i1.astype(F32),
          jnp.where(e == 1, i2.astype(F32),
          jnp.where(e == 2, g1,
          jnp.where(e == 3, g2,
          jnp.where(e == 4, r1,
          jnp.where(e == 5, r2, 0.0))))))
    o_ref[...] = out


def _router(x2, gain, w_router, tm=512):
    t, d = x2.shape
    wr = jnp.pad(w_router, ((0, 0), (0, LANES - N_EXPERTS)))
    return pl.pallas_call(
        functools.partial(_router_body, tm=tm),
        grid=(t // tm,),
        in_specs=[pl.BlockSpec((tm, d), lambda i: (i, 0)),
                  pl.BlockSpec((1, d), lambda i: (0, 0)),
                  pl.BlockSpec((d, LANES), lambda i: (0, 0))],
        out_specs=[pl.BlockSpec((N_EXPERTS, tm), lambda i: (0, i)),
                   pl.BlockSpec((N_EXPERTS, LANES), lambda i: (0, 0))],
        out_shape=[jax.ShapeDtypeStruct((N_EXPERTS, t), F32),
                   jax.ShapeDtypeStruct((N_EXPERTS, LANES), F32)],
        scratch_shapes=[pltpu.VMEM((N_EXPERTS, LANES), F32)],
        compiler_params=_params(("arbitrary",), 32),
        name="router_top2",
    )(x2, gain.reshape(1, d), wr)


def _dispatch_body(pos_ref, fill_ref, x_ref, g_ref, xs_ref, hbuf, zbuf, sem, zsem,
                   *, tk, tokens, n_tiles, n_steps):
    i = pl.program_id(0)
    slot = i % 2
    base = i * tk

    def tile_fill(tile):
        return pltpu.make_async_copy(zbuf, xs_ref.at[pl.ds(tile * MOE_TM, MOE_TM)], zsem)

    @pl.when(i == 0)
    def _():
        zbuf[...] = jnp.zeros_like(zbuf)

        def start_fill(tile, carry):
            @pl.when(fill_ref[tile] != 0)
            def _():
                tile_fill(tile).start()
            return carry

        lax.fori_loop(0, n_tiles, start_fill, 0)

        def wait_fill(tile, carry):
            @pl.when(fill_ref[tile] != 0)
            def _():
                tile_fill(tile).wait()
            return carry

        lax.fori_loop(0, n_tiles, wait_fill, 0)

    def wait_slot(s):
        for _ in range(2):
            pltpu.make_async_copy(hbuf.at[s], xs_ref.at[pl.ds(0, tk)], sem.at[s]).wait()

    @pl.when(i >= 2)
    def _():
        wait_slot(slot)

    hbuf[slot] = _normed(x_ref, g_ref)

    def start(group, carry):
        r0 = pl.multiple_of(group * 8, 8)
        for u in range(8):
            for assignment in range(2):
                dst = pos_ref[assignment * tokens + base + r0 + u]
                pltpu.make_async_copy(hbuf.at[slot, pl.ds(r0 + u, 1)], xs_ref.at[pl.ds(dst, 1)],
                                      sem.at[slot]).start()
        return carry

    lax.fori_loop(0, tk // 8, start, 0)

    @pl.when(i == n_steps - 1)
    def _():
        if n_steps >= 2:
            wait_slot(1 - slot)
        wait_slot(slot)


def _dispatch(x2, gain, pos_flat, fill_flag, n_tiles, tk=256):
    t, d = x2.shape
    return pl.pallas_call(
        functools.partial(_dispatch_body, tk=tk, tokens=t, n_tiles=n_tiles, n_steps=t // tk),
        grid_spec=pltpu.PrefetchScalarGridSpec(
            num_scalar_prefetch=2,
            grid=(t // tk,),
            in_specs=[pl.BlockSpec((tk, d), lambda i, pos, fill: (i, 0)),
                      pl.BlockSpec((1, d), lambda i, pos, fill: (0, 0))],
            out_specs=pl.BlockSpec(memory_space=pl.ANY),
            scratch_shapes=[pltpu.VMEM((2, tk, d), F32),
                            pltpu.VMEM((MOE_TM, d), F32),
                            pltpu.SemaphoreType.DMA((2,)),
                            pltpu.SemaphoreType.DMA(())]),
        out_shape=jax.ShapeDtypeStruct((n_tiles * MOE_TM, d), F32),
        compiler_params=_params(("arbitrary",), 32),
        name="moe_dispatch",
    )(pos_flat, fill_flag, x2, gain.reshape(1, d))


def _combine_body(pos_ref, x_ref, gates_ref, gf_ref, y_ref, o_ref, ybuf, sem,
                  *, tk, tokens, n_steps):
    i = pl.program_id(0)
    slot = i % 2

    def gather(step, s):
        base = step * tk

        def start(r, carry):
            for assignment in range(2):
                src = pos_ref[assignment * tokens + base + r]
                pltpu.make_async_copy(y_ref.at[pl.ds(src, 1)],
                                      ybuf.at[s, assignment, pl.ds(r, 1)], sem.at[s]).start()
            return carry

        lax.fori_loop(0, tk, start, 0, unroll=8)

    @pl.when(i == 0)
    def _():
        gather(0, 0)

    @pl.when(i + 1 < n_steps)
    def _():
        gather(i + 1, 1 - slot)

    for assignment in range(2):
        pltpu.make_async_copy(y_ref.at[pl.ds(0, tk)], ybuf.at[slot, assignment],
                              sem.at[slot]).wait()

    gates = gates_ref[...]
    x = x_ref[...] + gates[:, 0:1] * ybuf[slot, 0] + gates[:, 1:2] * ybuf[slot, 1]
    ms = jnp.mean(x * x, axis=-1, keepdims=True)
    o_ref[...] = x * lax.rsqrt(ms + EPS) * gf_ref[...]


def _combine(x2, y, pos_flat, gates, g_final, tk=256):
    t, d = x2.shape
    return pl.pallas_call(
        functools.partial(_combine_body, tk=tk, tokens=t, n_steps=t // tk),
        grid_spec=pltpu.PrefetchScalarGridSpec(
            num_scalar_prefetch=1,
            grid=(t // tk,),
            in_specs=[pl.BlockSpec((tk, d), lambda i, pos: (i, 0)),
                      pl.BlockSpec((tk, 2), lambda i, pos: (i, 0)),
                      pl.BlockSpec((1, d), lambda i, pos: (0, 0)),
                      pl.BlockSpec(memory_space=pl.ANY)],
            out_specs=pl.BlockSpec((tk, d), lambda i, pos: (i, 0)),
            scratch_shapes=[pltpu.VMEM((2, 2, tk, d), F32),
                            pltpu.SemaphoreType.DMA((2,))]),
        out_shape=jax.ShapeDtypeStruct((t, d), F32),
        compiler_params=_params(("arbitrary",), 32),
        name="moe_combine",
    )(pos_flat, x2, gates, g_final.reshape(1, d), y)


MOE_TM = 512
DENSE_TF = 512
EXPERT_TF = 1024
PROJ_TM = 1024
PROJ_TN = 1024


def _routing_tables(route, counts, n_tiles):
    idx1 = route[0].astype(jnp.int32)
    idx2 = route[1].astype(jnp.int32)
    cnt = counts[:, 0].astype(jnp.int32)
    tiles_e = (cnt + MOE_TM - 1) // MOE_TM
    tile_end = jnp.cumsum(tiles_e)
    offset = (tile_end - tiles_e) * MOE_TM
    pos1 = offset[idx1] + route[4].astype(jnp.int32)
    pos2 = offset[idx2] + route[5].astype(jnp.int32)
    n_used = tile_end[-1]
    tiles = jnp.arange(n_tiles, dtype=jnp.int32)
    te = jnp.sum((tiles[:, None] >= tile_end[None, :]).astype(jnp.int32), axis=1)
    te = jnp.minimum(te, N_EXPERTS - 1)
    te = jnp.where(tiles < n_used, te, te[n_used - 1])
    last_of_expert = jnp.any((tiles[:, None] == tile_end[None, :] - 1) & (tiles_e[None, :] > 0),
                             axis=1)
    fill_flag = (last_of_expert | (tiles >= n_used)).astype(jnp.int32)
    live = cnt[te] - (tiles - (tile_end - tiles_e)[te]) * MOE_TM
    tile_rows = jnp.where(tiles < n_used, jnp.clip(live, 0, MOE_TM), 0).astype(jnp.int32)
    pos_flat = jnp.concatenate([pos1, pos2]).astype(jnp.int32)
    gates = jnp.stack([route[2], route[3]], axis=1)
    return pos_flat, gates, te, n_used.reshape(1).astype(jnp.int32), fill_flag, tile_rows


def kernel(x, ln_mix, ln_ffn, ln_final, ret_w_in, ret_w_out, ffn_w_in, ffn_w_out,
           att_w_in, att_w_out, att_rel_bias, moe_router, moe_w_in, moe_w_out):
    b, s, d = x.shape
    t = b * s
    x2 = x.reshape(t, d)
    f_dense = ffn_w_out.shape[1]
    f_expert = moe_w_out.shape[2]

    moe_in2 = moe_w_in[0].reshape(N_EXPERTS * d, 2 * f_expert)
    moe_out2 = moe_w_out[0].reshape(N_EXPERTS * f_expert, d)
    ret_steps = b * RET_HEADS
    att_steps = ATT_HEADS * b
    cast_moe_gate = _SideCast(moe_in2, 128, f_expert, 0)
    cast_moe_up = _SideCast(moe_in2, att_steps, f_expert, 1)
    cast_moe_out = _SideCast(moe_out2, att_steps, d)
    cast_ffn_gate = _SideCast(ffn_w_in[0], ret_steps, f_dense, 0)
    cast_ffn_up = _SideCast(ffn_w_in[0], ret_steps, f_dense, 1)
    cast_ffn_out = _SideCast(ffn_w_out[0], ret_steps // 2, d)

    h = _rmsnorm(x2, ln_mix[0], BF16)
    proj, (moe_gate_bf,) = _matmul(h, ret_w_in[0], None, BF16, PROJ_TM, PROJ_TN, "ret_proj_in",
                                   sides=[cast_moe_gate])
    pos = jnp.arange(s, dtype=F32)
    inv_freq = ROPE_BASE ** (-jnp.arange(0, RET_QK_DIM, 2, dtype=F32) / RET_QK_DIM)
    ang = pos[:, None] * inv_freq[None, :]
    y, (ffn_gate_bf, ffn_up_bf, ffn_out_bf) = _retention(
        proj.reshape(b, s, -1), jnp.cos(ang), jnp.sin(ang),
        sides=[cast_ffn_gate, cast_ffn_up, cast_ffn_out])
    x2, _ = _matmul(y.reshape(t, RET_HV), ret_w_out[0], x2, F32, PROJ_TM, PROJ_TN // 2,
                    "ret_proj_out")

    dense_tiles = t // MOE_TM
    (x2, h), _ = _ffn(
        x2, (ln_ffn[0], ln_mix[1]), ffn_gate_bf[None], ffn_up_bf[None], ffn_out_bf[None],
        jnp.zeros((dense_tiles,), jnp.int32), jnp.full((1,), dense_tiles, jnp.int32),
        jnp.full((dense_tiles,), MOE_TM, jnp.int32),
        MOE_TM, DENSE_TF, 48, "dense_swiglu")

    qkv, _ = _matmul(h, att_w_in[0], None, BF16, PROJ_TM, PROJ_TN, "att_proj_in")
    a, (moe_up_bf, moe_out_bf) = _attention(qkv.reshape(b, s, -1), att_rel_bias[0],
                                            sides=[cast_moe_up, cast_moe_out])
    x2, _ = _matmul(a.reshape(t, d), att_w_out[0], x2, F32, PROJ_TM, PROJ_TN, "att_proj_out")

    route, counts = _router(x2, ln_ffn[1], moe_router[0])
    n_tiles = 2 * t // MOE_TM + N_EXPERTS
    pos_flat, gates, te, n_used, fill_flag, tile_rows = _routing_tables(route, counts, n_tiles)
    xs = _dispatch(x2, ln_ffn[1], pos_flat, fill_flag, n_tiles)
    (ys,), _ = _ffn(xs, None, moe_gate_bf.reshape(N_EXPERTS, d, f_expert),
                 moe_up_bf.reshape(N_EXPERTS, d, f_expert),
                 moe_out_bf.reshape(N_EXPERTS, f_expert, d), te, n_used, tile_rows,
                 MOE_TM, EXPERT_TF, 56, "expert_swiglu", half_tiles=True)
    out = _combine(x2, ys, pos_flat, gates, ln_final)
    return out.reshape(b, s, d)
```

```python
import functools
import math
from typing import NamedTuple

import jax
import jax.numpy as jnp
from jax import lax
from jax.experimental import pallas as pl
from jax.experimental.pallas import tpu as pltpu

D_MODEL = 2048
CHUNK = 64
EPS = 1e-6
RET_HEADS = 8
RET_QK_DIM = D_MODEL // RET_HEADS
RET_V_DIM = 2 * RET_QK_DIM
RET_HV = RET_HEADS * RET_V_DIM
ROPE_BASE = 10000.0
ATT_HEADS = 16
ATT_HEAD_DIM = D_MODEL // ATT_HEADS
LEFT_CHUNKS = 8
REL_CLIP = 128
NEG_INF = -1e30
N_EXPERTS = 8

LANES = 128
MIB = 1024 * 1024

F32 = jnp.float32
BF16 = jnp.bfloat16


def _params(semantics, vmem_mib):
    return pltpu.CompilerParams(
        dimension_semantics=semantics, vmem_limit_bytes=vmem_mib * MIB)


def _silu(x):
    return x * (1.0 / (1.0 + jnp.exp(-x)))


class _SideCast(NamedTuple):
    src: jax.Array
    num_blocks: int
    block_cols: int
    col_block: int = 0


def _call_with_sides(body, *, grid, in_specs, args, out_specs, out_shapes, scratch_shapes,
                     sides, params, name, prefetch=()):
    steps = math.prod(grid)
    n_pre, n_in, n_out, n_side = len(prefetch), len(args), len(out_shapes), len(sides)
    in_specs, args = list(in_specs), list(args)
    out_specs, out_shapes = list(out_specs), list(out_shapes)

    def block_spec(side, col_block):
        def index(*ids):
            step = ids[0]
            for extent, pid in zip(grid[1:], ids[1:len(grid)]):
                step = step * extent + pid
            return ((step * side.num_blocks) // steps, col_block)
        return pl.BlockSpec((side.src.shape[0] // side.num_blocks, side.block_cols), index)

    for side in sides:
        assert side.num_blocks <= steps and side.src.shape[0] % side.num_blocks == 0
        in_specs.append(block_spec(side, side.col_block))
        args.append(side.src)
        out_specs.append(block_spec(side, 0))
        out_shapes.append(jax.ShapeDtypeStruct((side.src.shape[0], side.block_cols), BF16))

    def wrapped(*refs):
        pre, refs = refs[:n_pre], refs[n_pre:]
        ins = refs[:n_in]
        srcs = refs[n_in:n_in + n_side]
        outs = refs[n_in + n_side:n_in + n_side + n_out]
        side_outs = refs[n_in + n_side + n_out:n_in + 2 * n_side + n_out]
        scratch = refs[n_in + 2 * n_side + n_out:]
        for src_ref, dst_ref in zip(srcs, side_outs):
            dst_ref[...] = src_ref[...].astype(BF16)
        body(*pre, *ins, *outs, *scratch)

    res = pl.pallas_call(
        wrapped,
        grid_spec=pltpu.PrefetchScalarGridSpec(
            num_scalar_prefetch=n_pre, grid=grid, in_specs=in_specs, out_specs=out_specs,
            scratch_shapes=scratch_shapes),
        out_shape=out_shapes, compiler_params=params, name=name)(*prefetch, *args)
    return res[:n_out], res[n_out:]


def _rmsnorm_body(x_ref, g_ref, o_ref):
    x = x_ref[...]
    ms = jnp.mean(x * x, axis=-1, keepdims=True)
    o_ref[...] = (x * lax.rsqrt(ms + EPS) * g_ref[...]).astype(o_ref.dtype)


def _rmsnorm(x2, g, out_dtype, tm=512):
    t, d = x2.shape
    return pl.pallas_call(
        _rmsnorm_body,
        grid=(t // tm,),
        in_specs=[pl.BlockSpec((tm, d), lambda i: (i, 0)),
                  pl.BlockSpec((1, d), lambda i: (0, 0))],
        out_specs=pl.BlockSpec((tm, d), lambda i: (i, 0)),
        out_shape=jax.ShapeDtypeStruct((t, d), out_dtype),
        compiler_params=_params(("arbitrary",), 32),
        name="rmsnorm",
    )(x2, g.reshape(1, d))


def _matmul_body(*refs, has_res):
    if has_res:
        x_ref, w_ref, r_ref, o_ref, wb_ref = refs
    else:
        x_ref, w_ref, o_ref, wb_ref = refs

    @pl.when(pl.program_id(1) == 0)
    def _():
        wb_ref[...] = w_ref[...].astype(BF16)

    acc = jnp.dot(x_ref[...], wb_ref[...], preferred_element_type=F32)
    if has_res:
        acc = acc + r_ref[...]
    o_ref[...] = acc.astype(o_ref.dtype)


def _matmul(x, w, res, out_dtype, tm, tn, name, sides=()):
    m, k = x.shape
    n = w.shape[1]
    in_specs = [pl.BlockSpec((tm, k), lambda j, i: (i, 0)),
                pl.BlockSpec((k, tn), lambda j, i: (0, j))]
    args = [x, w]
    if res is not None:
        in_specs.append(pl.BlockSpec((tm, tn), lambda j, i: (i, j)))
        args.append(res)
    (out,), side_outs = _call_with_sides(
        functools.partial(_matmul_body, has_res=res is not None),
        grid=(n // tn, m // tm),
        in_specs=in_specs, args=args,
        out_specs=[pl.BlockSpec((tm, tn), lambda j, i: (i, j))],
        out_shapes=[jax.ShapeDtypeStruct((m, n), out_dtype)],
        scratch_shapes=[pltpu.VMEM((k, tn), BF16)],
        sides=sides,
        params=_params(("arbitrary", "arbitrary"), 52),
        name=name)
    return out, side_outs


def _retention_body(q_ref, k_ref, v_ref, g_ref, cos_ref, sin_ref, o_ref, state_ref,
                    *, block_rows, seq):
    L = block_rows
    half = RET_QK_DIM // 2
    head = pl.program_id(1)

    def log_gamma(rows):
        headf = (jnp.zeros((rows, 1), jnp.int32) + head).astype(F32)
        return jnp.log(1.0 - jnp.exp2(-5.0 - headf))

    log_g = log_gamma(L)
    n_i = lax.broadcasted_iota(jnp.int32, (L, L), 0)
    m_i = lax.broadcasted_iota(jnp.int32, (L, L), 1)
    dist = jnp.abs(n_i - m_i).astype(F32)
    visible = (m_i >> 6) <= (n_i >> 6)
    decay = jnp.where(visible, jnp.exp(log_g * dist), 0.0)
    r = lax.broadcasted_iota(jnp.int32, (L, 1), 0).astype(F32)
    q_decay = jnp.exp(log_g * (r + 1.0))
    k_decay = jnp.exp(log_g * (L - 1.0 - r))
    block_decay = jnp.exp(log_gamma(RET_QK_DIM) * float(L))
    scale = RET_QK_DIM ** -0.5

    state_ref[...] = jnp.zeros_like(state_ref)

    def body(s, carry):
        r0 = pl.multiple_of(s * L, L)
        cos = cos_ref[pl.ds(r0, L), :]
        sin = sin_ref[pl.ds(r0, L), :]

        def rot(t):
            t1 = t[:, :half]
            t2 = t[:, half:]
            return jnp.concatenate([t1 * cos - t2 * sin, t1 * sin + t2 * cos], axis=1)

        q = rot(q_ref[pl.ds(r0, L), :].astype(F32)).astype(BF16)
        kf = rot(k_ref[pl.ds(r0, L), :].astype(F32)) * scale
        v = v_ref[pl.ds(r0, L), :]
        s_qk = lax.dot_general(q, kf.astype(BF16), (((1,), (1,)), ((), ())),
                               preferred_element_type=F32) * decay
        inner = jnp.dot(s_qk.astype(BF16), v, preferred_element_type=F32)
        st = state_ref[...]
        cross = jnp.dot(q, st.astype(BF16), preferred_element_type=F32) * q_decay
        kd_t = (kf * k_decay).T.astype(BF16)
        state_ref[...] = st * block_decay + jnp.dot(kd_t, v, preferred_element_type=F32)
        o = inner + cross
        o = o * lax.rsqrt(jnp.mean(o * o, axis=-1, keepdims=True) + EPS)
        gate = g_ref[pl.ds(r0, L), :].astype(F32)
        o_ref[pl.ds(r0, L), :] = (_silu(gate) * o).astype(o_ref.dtype)
        return carry

    lax.fori_loop(0, seq // L, body, 0, unroll=2)


def _retention(proj, cos, sin, sides=(), block_rows=256):
    b, s, _ = proj.shape
    nq = D_MODEL // RET_QK_DIM
    nv = RET_HV // RET_V_DIM
    (out,), side_outs = _call_with_sides(
        functools.partial(_retention_body, block_rows=block_rows, seq=s),
        grid=(b, RET_HEADS),
        in_specs=[
            pl.BlockSpec((None, s, RET_QK_DIM), lambda i, h: (i, 0, h)),
            pl.BlockSpec((None, s, RET_QK_DIM), lambda i, h: (i, 0, nq + h)),
            pl.BlockSpec((None, s, RET_V_DIM), lambda i, h: (i, 0, nv + h)),
            pl.BlockSpec((None, s, RET_V_DIM), lambda i, h: (i, 0, 2 * nv + h)),
            pl.BlockSpec((s, RET_QK_DIM // 2), lambda i, h: (0, 0)),
            pl.BlockSpec((s, RET_QK_DIM // 2), lambda i, h: (0, 0)),
        ],
        args=[proj, proj, proj, proj, cos, sin],
        out_specs=[pl.BlockSpec((None, s, RET_V_DIM), lambda i, h: (i, 0, h))],
        out_shapes=[jax.ShapeDtypeStruct((b, s, RET_HV), BF16)],
        scratch_shapes=[pltpu.VMEM((RET_QK_DIM, RET_V_DIM), F32)],
        sides=sides,
        params=_params(("arbitrary", "arbitrary"), 48),
        name="retention_core")
    return out, side_outs


ATT_PAD = LEFT_CHUNKS * CHUNK
ATT_LQ = 256
ATT_WIN = ATT_LQ + ATT_PAD
ATT_ROLL_W = ATT_LQ + ATT_WIN
REL_PAD = 384


def _attention_body(q_ref, k_ref, v_ref, rb_ref, o_ref, bm_ref, *, seq):
    @pl.when(pl.program_id(1) == 0)
    def _():
        t = lax.broadcasted_iota(jnp.int32, (REL_PAD, ATT_ROLL_W), 1)
        j = lax.broadcasted_iota(jnp.int32, (REL_PAD, ATT_ROLL_W), 0)
        idx = jnp.where(t <= ATT_WIN, jnp.clip(ATT_PAD - t, -REL_CLIP, REL_CLIP),
                        REL_CLIP) + REL_CLIP
        onehot = jnp.where(idx == j, 1.0, 0.0).astype(BF16)
        rb = rb_ref[...]
        hi = rb.astype(BF16)
        rest = rb - hi.astype(F32)
        mid = rest.astype(BF16)
        lo = (rest - mid.astype(F32)).astype(BF16)
        g = (jnp.dot(hi, onehot, preferred_element_type=F32)
             + jnp.dot(mid, onehot, preferred_element_type=F32)
             + jnp.dot(lo, onehot, preferred_element_type=F32))
        rows = jnp.broadcast_to(g[0:1, :], (ATT_LQ, ATT_ROLL_W))
        row = lax.broadcasted_iota(jnp.int32, (ATT_LQ, ATT_ROLL_W), 0)
        shift = 1
        while shift < ATT_LQ:
            rows = jnp.where((row & shift) != 0, pltpu.roll(rows, shift, 1), rows)
            shift *= 2
        qc = lax.broadcasted_iota(jnp.int32, (ATT_LQ, ATT_WIN), 0) >> 6
        kc = lax.broadcasted_iota(jnp.int32, (ATT_LQ, ATT_WIN), 1) >> 6
        ahead = kc - qc
        banded = jnp.where(ahead <= LEFT_CHUNKS, rows[:, :ATT_WIN], NEG_INF)
        bm_ref[...] = jnp.where(ahead >= 0, banded, NEG_INF)

    scale = ATT_HEAD_DIM ** -0.5
    for q0 in range(0, seq, ATT_LQ):
        k_lo = max(0, q0 - ATT_PAD)
        width = q0 + ATT_LQ - k_lo
        q = q_ref[q0:q0 + ATT_LQ, :]
        kw = k_ref[k_lo:k_lo + width, :]
        vw = v_ref[k_lo:k_lo + width, :]
        logits = lax.dot_general(q, kw, (((1,), (1,)), ((), ())),
                                 preferred_element_type=F32) * scale
        logits = logits + bm_ref[:, ATT_WIN - width:]
        mx = jnp.max(logits, axis=-1, keepdims=True)
        p = jnp.exp(logits - mx)
        denom = jnp.sum(p, axis=-1, keepdims=True)
        o = jnp.dot(p.astype(BF16), vw, preferred_element_type=F32) / denom
        o_ref[q0:q0 + ATT_LQ, :] = o.astype(o_ref.dtype)


def _attention(qkv, rel_bias, sides=()):
    b, s, _ = qkv.shape
    rb = jnp.zeros((ATT_HEADS, 8, REL_PAD), F32).at[:, 0, :2 * REL_CLIP + 1].set(rel_bias)
    (out,), side_outs = _call_with_sides(
        functools.partial(_attention_body, seq=s),
        grid=(ATT_HEADS, b),
        in_specs=[
            pl.BlockSpec((None, s, ATT_HEAD_DIM), lambda h, i: (i, 0, h)),
            pl.BlockSpec((None, s, ATT_HEAD_DIM), lambda h, i: (i, 0, ATT_HEADS + h)),
            pl.BlockSpec((None, s, ATT_HEAD_DIM), lambda h, i: (i, 0, 2 * ATT_HEADS + h)),
            pl.BlockSpec((None, 8, REL_PAD), lambda h, i: (h, 0, 0)),
        ],
        args=[qkv, qkv, qkv, rb],
        out_specs=[pl.BlockSpec((None, s, ATT_HEAD_DIM), lambda h, i: (i, 0, h))],
        out_shapes=[jax.ShapeDtypeStruct((b, s, D_MODEL), BF16)],
        scratch_shapes=[pltpu.VMEM((ATT_LQ, ATT_WIN), F32)],
        sides=sides,
        params=_params(("arbitrary", "arbitrary"), 48),
        name="chunk_attention")
    return out, side_outs


def _ffn_body(te_ref, nu_ref, tr_ref, *refs, normalize, half_tiles):
    del te_ref, nu_ref
    if normalize:
        x_ref, g_ref, g_next_ref, wg_ref, wu_ref, wo_ref, o_ref, h_next_ref, xb_ref = refs
    else:
        x_ref, wg_ref, wu_ref, wo_ref, o_ref, xb_ref = refs
    tm = xb_ref.shape[0]

    @pl.when(pl.program_id(1) == 0)
    def _():
        x = x_ref[...]
        if normalize:
            ms = jnp.mean(x * x, axis=-1, keepdims=True)
            xb_ref[...] = (x * lax.rsqrt(ms + EPS) * g_ref[...]).astype(BF16)
            o_ref[...] = x
        else:
            xb_ref[...] = x.astype(BF16)
            o_ref[...] = jnp.zeros_like(o_ref)

    def accumulate(m):
        xb = xb_ref[:m, :]
        gate = jnp.dot(xb, wg_ref[...], preferred_element_type=F32)
        up = jnp.dot(xb, wu_ref[...], preferred_element_type=F32)
        act = (_silu(gate) * up).astype(BF16)
        o_ref[:m, :] += jnp.dot(act, wo_ref[...], preferred_element_type=F32)

    rows_in_tile = tr_ref[pl.program_id(0)]
    if half_tiles:
        @pl.when(rows_in_tile > tm // 2)
        def _():
            accumulate(tm)

        @pl.when(jnp.logical_and(rows_in_tile > 0, rows_in_tile <= tm // 2))
        def _():
            accumulate(tm // 2)
    else:
        @pl.when(rows_in_tile > 0)
        def _():
            accumulate(tm)

    if normalize:
        @pl.when(pl.program_id(1) == pl.num_programs(1) - 1)
        def _():
            o = o_ref[...]
            ms = jnp.mean(o * o, axis=-1, keepdims=True)
            h_next_ref[...] = (o * lax.rsqrt(ms + EPS) * g_next_ref[...]).astype(BF16)


def _ffn(x, gains, w_gate, w_up, w_out, tile_expert, n_used, tile_rows, tm, tf, vmem_mib, name,
         half_tiles=False, sides=()):
    rows, d = x.shape
    f = w_out.shape[1]
    nj = f // tf
    nt = rows // tm
    normalize = gains is not None

    def col(i, j, nu):
        return jnp.where(i < nu[0], j, nj - 1)

    def row_block():
        return pl.BlockSpec((tm, d), lambda i, j, te, nu, tr: (i, 0))

    in_specs = [pl.BlockSpec(
        (tm, d), lambda i, j, te, nu, tr: (jnp.maximum(jnp.minimum(i, nu[0] - 1), 0), 0))]
    args = [x]
    out_specs = [row_block()]
    out_shapes = [jax.ShapeDtypeStruct((rows, d), F32)]
    if normalize:
        for gain in gains:
            in_specs.append(pl.BlockSpec((1, d), lambda i, j, te, nu, tr: (0, 0)))
            args.append(gain.reshape(1, d))
        out_specs.append(row_block())
        out_shapes.append(jax.ShapeDtypeStruct((rows, d), BF16))
    in_specs += [
        pl.BlockSpec((None, d, tf), lambda i, j, te, nu, tr: (te[i], 0, col(i, j, nu))),
        pl.BlockSpec((None, d, tf), lambda i, j, te, nu, tr: (te[i], 0, col(i, j, nu))),
        pl.BlockSpec((None, tf, d), lambda i, j, te, nu, tr: (te[i], col(i, j, nu), 0)),
    ]
    args += [w_gate, w_up, w_out]
    return _call_with_sides(
        functools.partial(_ffn_body, normalize=normalize, half_tiles=half_tiles),
        grid=(nt, nj),
        in_specs=in_specs, args=args,
        out_specs=out_specs, out_shapes=out_shapes,
        scratch_shapes=[pltpu.VMEM((tm, d), BF16)],
        sides=sides,
        params=_params(("arbitrary", "arbitrary"), vmem_mib),
        name=name,
        prefetch=(tile_expert, n_used, tile_rows))


def _normed(x_ref, g_ref):
    x = x_ref[...]
    ms = jnp.mean(x * x, axis=-1, keepdims=True)
    return x * lax.rsqrt(ms + EPS) * g_ref[...]


def _router_body(x_ref, g_ref, wr_ref, o_ref, cnt_ref, carry_ref, *, tm):
    @pl.when(pl.program_id(0) == 0)
    def _():
        carry_ref[...] = jnp.zeros_like(carry_ref)

    def split(a):
        hi = a.astype(BF16)
        return hi, (a - hi.astype(F32)).astype(BF16)

    h_hi, h_lo = split(_normed(x_ref, g_ref))
    w_hi, w_lo = split(wr_ref[...])
    logits = (jnp.dot(h_hi, w_hi, preferred_element_type=F32)
              + jnp.dot(h_hi, w_lo, preferred_element_type=F32)
              + jnp.dot(h_lo, w_hi, preferred_element_type=F32))
    lt = logits.T[:N_EXPERTS, :]
    e = lax.broadcasted_iota(jnp.int32, (N_EXPERTS, tm), 0)
    m1 = jnp.max(lt, axis=0, keepdims=True)
    i1 = jnp.min(jnp.where(lt == m1, e, N_EXPERTS), axis=0, keepdims=True)
    lt2 = jnp.where(e == i1, -jnp.inf, lt)
    m2 = jnp.max(lt2, axis=0, keepdims=True)
    i2 = jnp.min(jnp.where(lt2 == m2, e, N_EXPERTS), axis=0, keepdims=True)
    ex = jnp.exp(m2 - m1)
    g1 = 1.0 / (1.0 + ex)
    g2 = ex / (1.0 + ex)

    oh1 = jnp.where(e == i1, 1.0, 0.0)
    oh2 = jnp.where(e == i2, 1.0, 0.0)
    before = jnp.where(lax.broadcasted_iota(jnp.int32, (tm, tm), 0)
                       < lax.broadcasted_iota(jnp.int32, (tm, tm), 1), 1.0, 0.0).astype(BF16)
    c1 = jnp.dot(oh1.astype(BF16), before, preferred_element_type=F32)
    c2 = jnp.dot(oh2.astype(BF16), before, preferred_element_type=F32)
    t1 = jnp.sum(oh1, axis=1, keepdims=True)
    t2 = jnp.sum(oh2, axis=1, keepdims=True)
    carry = carry_ref[...]
    base = carry[:, :1]
    r1 = jnp.sum(oh1 * (base + c1), axis=0, keepdims=True)
    r2 = jnp.sum(oh2 * (base + t1 + c2), axis=0, keepdims=True)
    carry = carry + t1 + t2
    carry_ref[...] = carry
    cnt_ref[...] = carry

    out = jnp.where(e == 0, i1.astype(F32),
          jnp.where(e == 1, i2.astype(F32),
          jnp.where(e == 2, g1,
          jnp.where(e == 3, g2,
          jnp.where(e == 4, r1,
          jnp.where(e == 5, r2, 0.0))))))
    o_ref[...] = out


def _router(x2, gain, w_router, tm=512):
    t, d = x2.shape
    wr = jnp.pad(w_router, ((0, 0), (0, LANES - N_EXPERTS)))
    return pl.pallas_call(
        functools.partial(_router_body, tm=tm),
        grid=(t // tm,),
        in_specs=[pl.BlockSpec((tm, d), lambda i: (i, 0)),
                  pl.BlockSpec((1, d), lambda i: (0, 0)),
                  pl.BlockSpec((d, LANES), lambda i: (0, 0))],
        out_specs=[pl.BlockSpec((N_EXPERTS, tm), lambda i: (0, i)),
                   pl.BlockSpec((N_EXPERTS, LANES), lambda i: (0, 0))],
        out_shape=[jax.ShapeDtypeStruct((N_EXPERTS, t), F32),
                   jax.ShapeDtypeStruct((N_EXPERTS, LANES), F32)],
        scratch_shapes=[pltpu.VMEM((N_EXPERTS, LANES), F32)],
        compiler_params=_params(("arbitrary",), 32),
        name="router_top2",
    )(x2, gain.reshape(1, d), wr)


def _dispatch_body(pos_ref, fill_ref, x_ref, g_ref, xs_ref, hbuf, zbuf, sem, zsem,
                   *, tk, tokens, n_tiles, n_steps):
    i = pl.program_id(0)
    slot = i % 2
    base = i * tk

    def tile_fill(tile):
        return pltpu.make_async_copy(zbuf, xs_ref.at[pl.ds(tile * MOE_TM, MOE_TM)], zsem)

    @pl.when(i == 0)
    def _():
        zbuf[...] = jnp.zeros_like(zbuf)

        def start_fill(tile, carry):
            @pl.when(fill_ref[tile] != 0)
            def _():
                tile_fill(tile).start()
            return carry

        lax.fori_loop(0, n_tiles, start_fill, 0)

        def wait_fill(tile, carry):
            @pl.when(fill_ref[tile] != 0)
            def _():
                tile_fill(tile).wait()
            return carry

        lax.fori_loop(0, n_tiles, wait_fill, 0)

    def wait_slot(s):
        for _ in range(2):
            pltpu.make_async_copy(hbuf.at[s], xs_ref.at[pl.ds(0, tk)], sem.at[s]).wait()

    @pl.when(i >= 2)
    def _():
        wait_slot(slot)

    hbuf[slot] = _normed(x_ref, g_ref)

    def start(group, carry):
        r0 = pl.multiple_of(group * 8, 8)
        for u in range(8):
            for assignment in range(2):
                dst = pos_ref[assignment * tokens + base + r0 + u]
                pltpu.make_async_copy(hbuf.at[slot, pl.ds(r0 + u, 1)], xs_ref.at[pl.ds(dst, 1)],
                                      sem.at[slot]).start()
        return carry

    lax.fori_loop(0, tk // 8, start, 0)

    @pl.when(i == n_steps - 1)
    def _():
        if n_steps >= 2:
            wait_slot(1 - slot)
        wait_slot(slot)


def _dispatch(x2, gain, pos_flat, fill_flag, n_tiles, tk=256):
    t, d = x2.shape
    return pl.pallas_call(
        functools.partial(_dispatch_body, tk=tk, tokens=t, n_tiles=n_tiles, n_steps=t // tk),
        grid_spec=pltpu.PrefetchScalarGridSpec(
            num_scalar_prefetch=2,
            grid=(t // tk,),
            in_specs=[pl.BlockSpec((tk, d), lambda i, pos, fill: (i, 0)),
                      pl.BlockSpec((1, d), lambda i, pos, fill: (0, 0))],
            out_specs=pl.BlockSpec(memory_space=pl.ANY),
            scratch_shapes=[pltpu.VMEM((2, tk, d), F32),
                            pltpu.VMEM((MOE_TM, d), F32),
                            pltpu.SemaphoreType.DMA((2,)),
                            pltpu.SemaphoreType.DMA(())]),
        out_shape=jax.ShapeDtypeStruct((n_tiles * MOE_TM, d), F32),
        compiler_params=_params(("arbitrary",), 32),
        name="moe_dispatch",
    )(pos_flat, fill_flag, x2, gain.reshape(1, d))


def _combine_body(pos_ref, x_ref, gates_ref, gf_ref, y_ref, o_ref, ybuf, sem,
                  *, tk, tokens, n_steps):
    i = pl.program_id(0)
    slot = i % 2

    def gather(step, s):
        base = step * tk

        def start(r, carry):
            for assignment in range(2):
                src = pos_ref[assignment * tokens + base + r]
                pltpu.make_async_copy(y_ref.at[pl.ds(src, 1)],
                                      ybuf.at[s, assignment, pl.ds(r, 1)], sem.at[s]).start()
            return carry

        lax.fori_loop(0, tk, start, 0, unroll=8)

    @pl.when(i == 0)
    def _():
        gather(0, 0)

    @pl.when(i + 1 < n_steps)
    def _():
        gather(i + 1, 1 - slot)

    for assignment in range(2):
        pltpu.make_async_copy(y_ref.at[pl.ds(0, tk)], ybuf.at[slot, assignment],
                              sem.at[slot]).wait()

    gates = gates_ref[...]
    x = x_ref[...] + gates[:, 0:1] * ybuf[slot, 0] + gates[:, 1:2] * ybuf[slot, 1]
    ms = jnp.mean(x * x, axis=-1, keepdims=True)
    o_ref[...] = x * lax.rsqrt(ms + EPS) * gf_ref[...]


def _combine(x2, y, pos_flat, gates, g_final, tk=256):
    t, d = x2.shape
    return pl.pallas_call(
        functools.partial(_combine_body, tk=tk, tokens=t, n_steps=t // tk),
        grid_spec=pltpu.PrefetchScalarGridSpec(
            num_scalar_prefetch=1,
            grid=(t // tk,),
            in_specs=[pl.BlockSpec((tk, d), lambda i, pos: (i, 0)),
                      pl.BlockSpec((tk, 2), lambda i, pos: (i, 0)),
                      pl.BlockSpec((1, d), lambda i, pos: (0, 0)),
                      pl.BlockSpec(memory_space=pl.ANY)],
            out_specs=pl.BlockSpec((tk, d), lambda i, pos: (i, 0)),
            scratch_shapes=[pltpu.VMEM((2, 2, tk, d), F32),
                            pltpu.SemaphoreType.DMA((2,))]),
        out_shape=jax.ShapeDtypeStruct((t, d), F32),
        compiler_params=_params(("arbitrary",), 32),
        name="moe_combine",
    )(pos_flat, x2, gates, g_final.reshape(1, d), y)


MOE_TM = 512
DENSE_TF = 512
EXPERT_TF = 1024
PROJ_TM = 1024
PROJ_TN = 1024


def _routing_tables(route, counts, n_tiles):
    idx1 = route[0].astype(jnp.int32)
    idx2 = route[1].astype(jnp.int32)
    cnt = counts[:, 0].astype(jnp.int32)
    tiles_e = (cnt + MOE_TM - 1) // MOE_TM
    tile_end = jnp.cumsum(tiles_e)
    offset = (tile_end - tiles_e) * MOE_TM
    pos1 = offset[idx1] + route[4].astype(jnp.int32)
    pos2 = offset[idx2] + route[5].astype(jnp.int32)
    n_used = tile_end[-1]
    tiles = jnp.arange(n_tiles, dtype=jnp.int32)
    te = jnp.sum((tiles[:, None] >= tile_end[None, :]).astype(jnp.int32), axis=1)
    te = jnp.minimum(te, N_EXPERTS - 1)
    te = jnp.where(tiles < n_used, te, te[n_used - 1])
    last_of_expert = jnp.any((tiles[:, None] == tile_end[None, :] - 1) & (tiles_e[None, :] > 0),
                             axis=1)
    fill_flag = (last_of_expert | (tiles >= n_used)).astype(jnp.int32)
    live = cnt[te] - (tiles - (tile_end - tiles_e)[te]) * MOE_TM
    tile_rows = jnp.where(tiles < n_used, jnp.clip(live, 0, MOE_TM), 0).astype(jnp.int32)
    pos_flat = jnp.concatenate([pos1, pos2]).astype(jnp.int32)
    gates = jnp.stack([route[2], route[3]], axis=1)
    return pos_flat, gates, te, n_used.reshape(1).astype(jnp.int32), fill_flag, tile_rows


def kernel(x, ln_mix, ln_ffn, ln_final, ret_w_in, ret_w_out, ffn_w_in, ffn_w_out,
           att_w_in, att_w_out, att_rel_bias, moe_router, moe_w_in, moe_w_out):
    b, s, d = x.shape
    t = b * s
    x2 = x.reshape(t, d)
    f_dense = ffn_w_out.shape[1]
    f_expert = moe_w_out.shape[2]

    moe_in2 = moe_w_in[0].reshape(N_EXPERTS * d, 2 * f_expert)
    moe_out2 = moe_w_out[0].reshape(N_EXPERTS * f_expert, d)
    ret_steps = b * RET_HEADS
    att_steps = ATT_HEADS * b
    cast_moe_gate = _SideCast(moe_in2, 128, f_expert, 0)
    cast_moe_up = _SideCast(moe_in2, att_steps, f_expert, 1)
    cast_moe_out = _SideCast(moe_out2, 256, d)
    cast_ffn_gate = _SideCast(ffn_w_in[0], ret_steps, f_dense, 0)
    cast_ffn_up = _SideCast(ffn_w_in[0], ret_steps, f_dense, 1)
    cast_ffn_out = _SideCast(ffn_w_out[0], ret_steps // 2, d)

    h = _rmsnorm(x2, ln_mix[0], BF16)
    proj, (moe_gate_bf,) = _matmul(h, ret_w_in[0], None, BF16, PROJ_TM, PROJ_TN, "ret_proj_in",
                                   sides=[cast_moe_gate])
    pos = jnp.arange(s, dtype=F32)
    inv_freq = ROPE_BASE ** (-jnp.arange(0, RET_QK_DIM, 2, dtype=F32) / RET_QK_DIM)
    ang = pos[:, None] * inv_freq[None, :]
    y, (ffn_gate_bf, ffn_up_bf, ffn_out_bf) = _retention(
        proj.reshape(b, s, -1), jnp.cos(ang), jnp.sin(ang),
        sides=[cast_ffn_gate, cast_ffn_up, cast_ffn_out])
    x2, _ = _matmul(y.reshape(t, RET_HV), ret_w_out[0], x2, F32, PROJ_TM, PROJ_TN // 2,
                    "ret_proj_out")

    dense_tiles = t // MOE_TM
    (x2, h), (moe_out_bf,) = _ffn(
        x2, (ln_ffn[0], ln_mix[1]), ffn_gate_bf[None], ffn_up_bf[None], ffn_out_bf[None],
        jnp.zeros((dense_tiles,), jnp.int32), jnp.full((1,), dense_tiles, jnp.int32),
        jnp.full((dense_tiles,), MOE_TM, jnp.int32),
        MOE_TM, DENSE_TF, 48, "dense_swiglu", sides=[cast_moe_out])

    qkv, _ = _matmul(h, att_w_in[0], None, BF16, PROJ_TM, PROJ_TN, "att_proj_in")
    a, (moe_up_bf,) = _attention(qkv.reshape(b, s, -1), att_rel_bias[0], sides=[cast_moe_up])
    x2, _ = _matmul(a.reshape(t, d), att_w_out[0], x2, F32, PROJ_TM, PROJ_TN, "att_proj_out")

    route, counts = _router(x2, ln_ffn[1], moe_router[0])
    n_tiles = 2 * t // MOE_TM + N_EXPERTS
    pos_flat, gates, te, n_used, fill_flag, tile_rows = _routing_tables(route, counts, n_tiles)
    xs = _dispatch(x2, ln_ffn[1], pos_flat, fill_flag, n_tiles)
    (ys,), _ = _ffn(xs, None, moe_gate_bf.reshape(N_EXPERTS, d, f_expert),
                 moe_up_bf.reshape(N_EXPERTS, d, f_expert),
                 moe_out_bf.reshape(N_EXPERTS, f_expert, d), te, n_used, tile_rows,
                 MOE_TM, EXPERT_TF, 56, "expert_swiglu", half_tiles=True)
    out = _combine(x2, ys, pos_flat, gates, ln_final)
    return out.reshape(b, s, d)
```

```python
import functools
import math
from typing import NamedTuple

import jax
import jax.numpy as jnp
from jax import lax
from jax.experimental import pallas as pl
from jax.experimental.pallas import tpu as pltpu

D_MODEL = 2048
CHUNK = 64
EPS = 1e-6
RET_HEADS = 8
RET_QK_DIM = D_MODEL // RET_HEADS
RET_V_DIM = 2 * RET_QK_DIM
RET_HV = RET_HEADS * RET_V_DIM
ROPE_BASE = 10000.0
ATT_HEADS = 16
ATT_HEAD_DIM = D_MODEL // ATT_HEADS
LEFT_CHUNKS = 8
REL_CLIP = 128
NEG_INF = -1e30
N_EXPERTS = 8

LANES = 128
MIB = 1024 * 1024

F32 = jnp.float32
BF16 = jnp.bfloat16


def _params(semantics, vmem_mib):
    return pltpu.CompilerParams(
        dimension_semantics=semantics, vmem_limit_bytes=vmem_mib * MIB)


def _silu(x):
    return x * (1.0 / (1.0 + jnp.exp(-x)))


class _SideCast(NamedTuple):
    src: jax.Array
    num_blocks: int
    block_cols: int
    col_block: int = 0


def _call_with_sides(body, *, grid, in_specs, args, out_specs, out_shapes, scratch_shapes,
                     sides, params, name, prefetch=()):
    steps = math.prod(grid)
    n_pre, n_in, n_out, n_side = len(prefetch), len(args), len(out_shapes), len(sides)
    in_specs, args = list(in_specs), list(args)
    out_specs, out_shapes = list(out_specs), list(out_shapes)

    def block_spec(side, col_block):
        def index(*ids):
            step = ids[0]
            for extent, pid in zip(grid[1:], ids[1:len(grid)]):
                step = step * extent + pid
            return ((step * side.num_blocks) // steps, col_block)
        return pl.BlockSpec((side.src.shape[0] // side.num_blocks, side.block_cols), index)

    for side in sides:
        assert side.num_blocks <= steps and side.src.shape[0] % side.num_blocks == 0
        in_specs.append(block_spec(side, side.col_block))
        args.append(side.src)
        out_specs.append(block_spec(side, 0))
        out_shapes.append(jax.ShapeDtypeStruct((side.src.shape[0], side.block_cols), BF16))

    def wrapped(*refs):
        pre, refs = refs[:n_pre], refs[n_pre:]
        ins = refs[:n_in]
        srcs = refs[n_in:n_in + n_side]
        outs = refs[n_in + n_side:n_in + n_side + n_out]
        side_outs = refs[n_in + n_side + n_out:n_in + 2 * n_side + n_out]
        scratch = refs[n_in + 2 * n_side + n_out:]
        for src_ref, dst_ref in zip(srcs, side_outs):
            dst_ref[...] = src_ref[...].astype(BF16)
        body(*pre, *ins, *outs, *scratch)

    res = pl.pallas_call(
        wrapped,
        grid_spec=pltpu.PrefetchScalarGridSpec(
            num_scalar_prefetch=n_pre, grid=grid, in_specs=in_specs, out_specs=out_specs,
            scratch_shapes=scratch_shapes),
        out_shape=out_shapes, compiler_params=params, name=name)(*prefetch, *args)
    return res[:n_out], res[n_out:]


def _rmsnorm_body(x_ref, g_ref, o_ref):
    x = x_ref[...]
    ms = jnp.mean(x * x, axis=-1, keepdims=True)
    o_ref[...] = (x * lax.rsqrt(ms + EPS) * g_ref[...]).astype(o_ref.dtype)


def _rmsnorm(x2, g, out_dtype, tm=512):
    t, d = x2.shape
    return pl.pallas_call(
        _rmsnorm_body,
        grid=(t // tm,),
        in_specs=[pl.BlockSpec((tm, d), lambda i: (i, 0)),
                  pl.BlockSpec((1, d), lambda i: (0, 0))],
        out_specs=pl.BlockSpec((tm, d), lambda i: (i, 0)),
        out_shape=jax.ShapeDtypeStruct((t, d), out_dtype),
        compiler_params=_params(("arbitrary",), 32),
        name="rmsnorm",
    )(x2, g.reshape(1, d))


def _matmul_body(*refs, has_res):
    if has_res:
        x_ref, w_ref, r_ref, o_ref, wb_ref = refs
    else:
        x_ref, w_ref, o_ref, wb_ref = refs

    @pl.when(pl.program_id(1) == 0)
    def _():
        wb_ref[...] = w_ref[...].astype(BF16)

    acc = jnp.dot(x_ref[...], wb_ref[...], preferred_element_type=F32)
    if has_res:
        acc = acc + r_ref[...]
    o_ref[...] = acc.astype(o_ref.dtype)


def _matmul(x, w, res, out_dtype, tm, tn, name, sides=()):
    m, k = x.shape
    n = w.shape[1]
    in_specs = [pl.BlockSpec((tm, k), lambda j, i: (i, 0)),
                pl.BlockSpec((k, tn), lambda j, i: (0, j))]
    args = [x, w]
    if res is not None:
        in_specs.append(pl.BlockSpec((tm, tn), lambda j, i: (i, j)))
        args.append(res)
    (out,), side_outs = _call_with_sides(
        functools.partial(_matmul_body, has_res=res is not None),
        grid=(n // tn, m // tm),
        in_specs=in_specs, args=args,
        out_specs=[pl.BlockSpec((tm, tn), lambda j, i: (i, j))],
        out_shapes=[jax.ShapeDtypeStruct((m, n), out_dtype)],
        scratch_shapes=[pltpu.VMEM((k, tn), BF16)],
        sides=sides,
        params=_params(("arbitrary", "arbitrary"), 52),
        name=name)
    return out, side_outs


def _retention_body(q_ref, k_ref, v_ref, g_ref, cos_ref, sin_ref, o_ref, state_ref,
                    *, block_rows, seq):
    L = block_rows
    half = RET_QK_DIM // 2
    head = pl.program_id(1)

    def log_gamma(rows):
        headf = (jnp.zeros((rows, 1), jnp.int32) + head).astype(F32)
        return jnp.log(1.0 - jnp.exp2(-5.0 - headf))

    log_g = log_gamma(L)
    n_i = lax.broadcasted_iota(jnp.int32, (L, L), 0)
    m_i = lax.broadcasted_iota(jnp.int32, (L, L), 1)
    dist = jnp.abs(n_i - m_i).astype(F32)
    visible = (m_i >> 6) <= (n_i >> 6)
    decay = jnp.where(visible, jnp.exp(log_g * dist), 0.0)
    r = lax.broadcasted_iota(jnp.int32, (L, 1), 0).astype(F32)
    q_decay = jnp.exp(log_g * (r + 1.0))
    k_decay = jnp.exp(log_g * (L - 1.0 - r))
    block_decay = jnp.exp(log_gamma(RET_QK_DIM) * float(L))
    scale = RET_QK_DIM ** -0.5

    state_ref[...] = jnp.zeros_like(state_ref)

    def body(s, carry):
        r0 = pl.multiple_of(s * L, L)
        cos = cos_ref[pl.ds(r0, L), :]
        sin = sin_ref[pl.ds(r0, L), :]

        def rot(t):
            t1 = t[:, :half]
            t2 = t[:, half:]
            return jnp.concatenate([t1 * cos - t2 * sin, t1 * sin + t2 * cos], axis=1)

        q = rot(q_ref[pl.ds(r0, L), :].astype(F32)).astype(BF16)
        kf = rot(k_ref[pl.ds(r0, L), :].astype(F32)) * scale
        v = v_ref[pl.ds(r0, L), :]
        s_qk = lax.dot_general(q, kf.astype(BF16), (((1,), (1,)), ((), ())),
                               preferred_element_type=F32) * decay
        inner = jnp.dot(s_qk.astype(BF16), v, preferred_element_type=F32)
        st = state_ref[...]
        cross = jnp.dot(q, st.astype(BF16), preferred_element_type=F32) * q_decay
        kd_t = (kf * k_decay).T.astype(BF16)
        state_ref[...] = st * block_decay + jnp.dot(kd_t, v, preferred_element_type=F32)
        o = inner + cross
        o = o * lax.rsqrt(jnp.mean(o * o, axis=-1, keepdims=True) + EPS)
        gate = g_ref[pl.ds(r0, L), :].astype(F32)
        o_ref[pl.ds(r0, L), :] = (_silu(gate) * o).astype(o_ref.dtype)
        return carry

    lax.fori_loop(0, seq // L, body, 0, unroll=2)


def _retention(proj, cos, sin, sides=(), block_rows=256):
    b, s, _ = proj.shape
    nq = D_MODEL // RET_QK_DIM
    nv = RET_HV // RET_V_DIM
    (out,), side_outs = _call_with_sides(
        functools.partial(_retention_body, block_rows=block_rows, seq=s),
        grid=(b, RET_HEADS),
        in_specs=[
            pl.BlockSpec((None, s, RET_QK_DIM), lambda i, h: (i, 0, h)),
            pl.BlockSpec((None, s, RET_QK_DIM), lambda i, h: (i, 0, nq + h)),
            pl.BlockSpec((None, s, RET_V_DIM), lambda i, h: (i, 0, nv + h)),
            pl.BlockSpec((None, s, RET_V_DIM), lambda i, h: (i, 0, 2 * nv + h)),
            pl.BlockSpec((s, RET_QK_DIM // 2), lambda i, h: (0, 0)),
            pl.BlockSpec((s, RET_QK_DIM // 2), lambda i, h: (0, 0)),
        ],
        args=[proj, proj, proj, proj, cos, sin],
        out_specs=[pl.BlockSpec((None, s, RET_V_DIM), lambda i, h: (i, 0, h))],
        out_shapes=[jax.ShapeDtypeStruct((b, s, RET_HV), BF16)],
        scratch_shapes=[pltpu.VMEM((RET_QK_DIM, RET_V_DIM), F32)],
        sides=sides,
        params=_params(("arbitrary", "arbitrary"), 48),
        name="retention_core")
    return out, side_outs


ATT_PAD = LEFT_CHUNKS * CHUNK
ATT_LQ = 256
ATT_WIN = ATT_LQ + ATT_PAD
ATT_ROLL_W = ATT_LQ + ATT_WIN
REL_PAD = 384


ATT_HPS = 2
ATT_GROUPS = ATT_HEADS // ATT_HPS


def _attention_body(q_ref, k_ref, v_ref, rb_ref, o_ref, bm_ref, *, seq):
    @pl.when(pl.program_id(1) == 0)
    def _():
        t = lax.broadcasted_iota(jnp.int32, (REL_PAD, ATT_ROLL_W), 1)
        j = lax.broadcasted_iota(jnp.int32, (REL_PAD, ATT_ROLL_W), 0)
        idx = jnp.where(t <= ATT_WIN, jnp.clip(ATT_PAD - t, -REL_CLIP, REL_CLIP),
                        REL_CLIP) + REL_CLIP
        onehot = jnp.where(idx == j, 1.0, 0.0).astype(BF16)
        row = lax.broadcasted_iota(jnp.int32, (ATT_LQ, ATT_ROLL_W), 0)
        qc = lax.broadcasted_iota(jnp.int32, (ATT_LQ, ATT_WIN), 0) >> 6
        kc = lax.broadcasted_iota(jnp.int32, (ATT_LQ, ATT_WIN), 1) >> 6
        ahead = kc - qc
        for hh in range(ATT_HPS):
            rb = rb_ref[hh]
            hi = rb.astype(BF16)
            rest = rb - hi.astype(F32)
            mid = rest.astype(BF16)
            lo = (rest - mid.astype(F32)).astype(BF16)
            g = (jnp.dot(hi, onehot, preferred_element_type=F32)
                 + jnp.dot(mid, onehot, preferred_element_type=F32)
                 + jnp.dot(lo, onehot, preferred_element_type=F32))
            rows = jnp.broadcast_to(g[0:1, :], (ATT_LQ, ATT_ROLL_W))
            shift = 1
            while shift < ATT_LQ:
                rows = jnp.where((row & shift) != 0, pltpu.roll(rows, shift, 1), rows)
                shift *= 2
            banded = jnp.where(ahead <= LEFT_CHUNKS, rows[:, :ATT_WIN], NEG_INF)
            bm_ref[hh] = jnp.where(ahead >= 0, banded, NEG_INF)

    scale = ATT_HEAD_DIM ** -0.5
    for q0 in range(0, seq, ATT_LQ):
        k_lo = max(0, q0 - ATT_PAD)
        width = q0 + ATT_LQ - k_lo
        for hh in range(ATT_HPS):
            cols = slice(hh * ATT_HEAD_DIM, (hh + 1) * ATT_HEAD_DIM)
            q = q_ref[q0:q0 + ATT_LQ, cols]
            kw = k_ref[k_lo:k_lo + width, cols]
            vw = v_ref[k_lo:k_lo + width, cols]
            logits = lax.dot_general(q, kw, (((1,), (1,)), ((), ())),
                                     preferred_element_type=F32) * scale
            logits = logits + bm_ref[hh, :, ATT_WIN - width:]
            mx = jnp.max(logits, axis=-1, keepdims=True)
            p = jnp.exp(logits - mx)
            denom = jnp.sum(p, axis=-1, keepdims=True)
            o = jnp.dot(p.astype(BF16), vw, preferred_element_type=F32) / denom
            o_ref[q0:q0 + ATT_LQ, cols] = o.astype(o_ref.dtype)


def _attention(qkv, rel_bias, sides=()):
    b, s, _ = qkv.shape
    rb = jnp.zeros((ATT_HEADS, 8, REL_PAD), F32).at[:, 0, :2 * REL_CLIP + 1].set(rel_bias)
    wide = ATT_HPS * ATT_HEAD_DIM
    (out,), side_outs = _call_with_sides(
        functools.partial(_attention_body, seq=s),
        grid=(ATT_GROUPS, b),
        in_specs=[
            pl.BlockSpec((None, s, wide), lambda h, i: (i, 0, h)),
            pl.BlockSpec((None, s, wide), lambda h, i: (i, 0, ATT_GROUPS + h)),
            pl.BlockSpec((None, s, wide), lambda h, i: (i, 0, 2 * ATT_GROUPS + h)),
            pl.BlockSpec((ATT_HPS, 8, REL_PAD), lambda h, i: (h, 0, 0)),
        ],
        args=[qkv, qkv, qkv, rb],
        out_specs=[pl.BlockSpec((None, s, wide), lambda h, i: (i, 0, h))],
        out_shapes=[jax.ShapeDtypeStruct((b, s, D_MODEL), BF16)],
        scratch_shapes=[pltpu.VMEM((ATT_HPS, ATT_LQ, ATT_WIN), F32)],
        sides=sides,
        params=_params(("arbitrary", "arbitrary"), 48),
        name="chunk_attention")
    return out, side_outs


def _ffn_body(te_ref, nu_ref, tr_ref, *refs, normalize, half_tiles):
    del te_ref, nu_ref
    if normalize:
        x_ref, g_ref, g_next_ref, wg_ref, wu_ref, wo_ref, o_ref, h_next_ref, xb_ref = refs
    else:
        x_ref, wg_ref, wu_ref, wo_ref, o_ref, xb_ref = refs
    tm = xb_ref.shape[0]

    @pl.when(pl.program_id(1) == 0)
    def _():
        x = x_ref[...]
        if normalize:
            ms = jnp.mean(x * x, axis=-1, keepdims=True)
            xb_ref[...] = (x * lax.rsqrt(ms + EPS) * g_ref[...]).astype(BF16)
            o_ref[...] = x
        else:
            xb_ref[...] = x.astype(BF16)
            o_ref[...] = jnp.zeros_like(o_ref)

    def accumulate(m):
        xb = xb_ref[:m, :]
        gate = jnp.dot(xb, wg_ref[...], preferred_element_type=F32)
        up = jnp.dot(xb, wu_ref[...], preferred_element_type=F32)
        act = (_silu(gate) * up).astype(BF16)
        o_ref[:m, :] += jnp.dot(act, wo_ref[...], preferred_element_type=F32)

    rows_in_tile = tr_ref[pl.program_id(0)]
    if half_tiles:
        @pl.when(rows_in_tile > tm // 2)
        def _():
            accumulate(tm)

        @pl.when(jnp.logical_and(rows_in_tile > 0, rows_in_tile <= tm // 2))
        def _():
            accumulate(tm // 2)
    else:
        @pl.when(rows_in_tile > 0)
        def _():
            accumulate(tm)

    if normalize:
        @pl.when(pl.program_id(1) == pl.num_programs(1) - 1)
        def _():
            o = o_ref[...]
            ms = jnp.mean(o * o, axis=-1, keepdims=True)
            h_next_ref[...] = (o * lax.rsqrt(ms + EPS) * g_next_ref[...]).astype(BF16)


def _ffn(x, gains, w_gate, w_up, w_out, tile_expert, n_used, tile_rows, tm, tf, vmem_mib, name,
         half_tiles=False, sides=()):
    rows, d = x.shape
    f = w_out.shape[1]
    nj = f // tf
    nt = rows // tm
    normalize = gains is not None

    def col(i, j, nu):
        return jnp.where(i < nu[0], j, nj - 1)

    def row_block():
        return pl.BlockSpec((tm, d), lambda i, j, te, nu, tr: (i, 0))

    in_specs = [pl.BlockSpec(
        (tm, d), lambda i, j, te, nu, tr: (jnp.maximum(jnp.minimum(i, nu[0] - 1), 0), 0))]
    args = [x]
    out_specs = [row_block()]
    out_shapes = [jax.ShapeDtypeStruct((rows, d), F32)]
    if normalize:
        for gain in gains:
            in_specs.append(pl.BlockSpec((1, d), lambda i, j, te, nu, tr: (0, 0)))
            args.append(gain.reshape(1, d))
        out_specs.append(row_block())
        out_shapes.append(jax.ShapeDtypeStruct((rows, d), BF16))
    in_specs += [
        pl.BlockSpec((None, d, tf), lambda i, j, te, nu, tr: (te[i], 0, col(i, j, nu))),
        pl.BlockSpec((None, d, tf), lambda i, j, te, nu, tr: (te[i], 0, col(i, j, nu))),
        pl.BlockSpec((None, tf, d), lambda i, j, te, nu, tr: (te[i], col(i, j, nu), 0)),
    ]
    args += [w_gate, w_up, w_out]
    return _call_with_sides(
        functools.partial(_ffn_body, normalize=normalize, half_tiles=half_tiles),
        grid=(nt, nj),
        in_specs=in_specs, args=args,
        out_specs=out_specs, out_shapes=out_shapes,
        scratch_shapes=[pltpu.VMEM((tm, d), BF16)],
        sides=sides,
        params=_params(("arbitrary", "arbitrary"), vmem_mib),
        name=name,
        prefetch=(tile_expert, n_used, tile_rows))


def _normed(x_ref, g_ref):
    x = x_ref[...]
    ms = jnp.mean(x * x, axis=-1, keepdims=True)
    return x * lax.rsqrt(ms + EPS) * g_ref[...]


def _router_body(x_ref, g_ref, wr_ref, o_ref, cnt_ref, carry_ref, *, tm):
    @pl.when(pl.program_id(0) == 0)
    def _():
        carry_ref[...] = jnp.zeros_like(carry_ref)

    def split(a):
        hi = a.astype(BF16)
        return hi, (a - hi.astype(F32)).astype(BF16)

    h_hi, h_lo = split(_normed(x_ref, g_ref))
    w_hi, w_lo = split(wr_ref[...])
    logits = (jnp.dot(h_hi, w_hi, preferred_element_type=F32)
              + jnp.dot(h_hi, w_lo, preferred_element_type=F32)
              + jnp.dot(h_lo, w_hi, preferred_element_type=F32))
    lt = logits.T[:N_EXPERTS, :]
    e = lax.broadcasted_iota(jnp.int32, (N_EXPERTS, tm), 0)
    m1 = jnp.max(lt, axis=0, keepdims=True)
    i1 = jnp.min(jnp.where(lt == m1, e, N_EXPERTS), axis=0, keepdims=True)
    lt2 = jnp.where(e == i1, -jnp.inf, lt)
    m2 = jnp.max(lt2, axis=0, keepdims=True)
    i2 = jnp.min(jnp.where(lt2 == m2, e, N_EXPERTS), axis=0, keepdims=True)
    ex = jnp.exp(m2 - m1)
    g1 = 1.0 / (1.0 + ex)
    g2 = ex / (1.0 + ex)

    oh1 = jnp.where(e == i1, 1.0, 0.0)
    oh2 = jnp.where(e == i2, 1.0, 0.0)
    before = jnp.where(lax.broadcasted_iota(jnp.int32, (tm, tm), 0)
                       < lax.broadcasted_iota(jnp.int32, (tm, tm), 1), 1.0, 0.0).astype(BF16)
    c1 = jnp.dot(oh1.astype(BF16), before, preferred_element_type=F32)
    c2 = jnp.dot(oh2.astype(BF16), before, preferred_element_type=F32)
    t1 = jnp.sum(oh1, axis=1, keepdims=True)
    t2 = jnp.sum(oh2, axis=1, keepdims=True)
    carry = carry_ref[...]
    base = carry[:, :1]
    r1 = jnp.sum(oh1 * (base + c1), axis=0, keepdims=True)
    r2 = jnp.sum(oh2 * (base + t1 + c2), axis=0, keepdims=True)
    carry = carry + t1 + t2
    carry_ref[...] = carry
    cnt_ref[...] = carry

    out = jnp.where(e == 0, i1.astype(F32),
          jnp.where(e == 1, i2.astype(F32),
          jnp.where(e == 2, g1,
          jnp.where(e == 3, g2,
          jnp.where(e == 4, r1,
          jnp.where(e == 5, r2, 0.0))))))
    o_ref[...] = out


def _router(x2, gain, w_router, tm=512):
    t, d = x2.shape
    wr = jnp.pad(w_router, ((0, 0), (0, LANES - N_EXPERTS)))
    return pl.pallas_call(
        functools.partial(_router_body, tm=tm),
        grid=(t // tm,),
        in_specs=[pl.BlockSpec((tm, d), lambda i: (i, 0)),
                  pl.BlockSpec((1, d), lambda i: (0, 0)),
                  pl.BlockSpec((d, LANES), lambda i: (0, 0))],
        out_specs=[pl.BlockSpec((N_EXPERTS, tm), lambda i: (0, i)),
                   pl.BlockSpec((N_EXPERTS, LANES), lambda i: (0, 0))],
        out_shape=[jax.ShapeDtypeStruct((N_EXPERTS, t), F32),
                   jax.ShapeDtypeStruct((N_EXPERTS, LANES), F32)],
        scratch_shapes=[pltpu.VMEM((N_EXPERTS, LANES), F32)],
        compiler_params=_params(("arbitrary",), 32),
        name="router_top2",
    )(x2, gain.reshape(1, d), wr)


def _dispatch_body(pos_ref, fill_ref, x_ref, g_ref, xs_ref, hbuf, zbuf, sem, zsem,
                   *, tk, tokens, n_tiles, n_steps):
    i = pl.program_id(0)
    slot = i % 2
    base = i * tk

    def tile_fill(tile):
        return pltpu.make_async_copy(zbuf, xs_ref.at[pl.ds(tile * MOE_TM, MOE_TM)], zsem)

    @pl.when(i == 0)
    def _():
        zbuf[...] = jnp.zeros_like(zbuf)

        def start_fill(tile, carry):
            @pl.when(fill_ref[tile] != 0)
            def _():
                tile_fill(tile).start()
            return carry

        lax.fori_loop(0, n_tiles, start_fill, 0)

        def wait_fill(tile, carry):
            @pl.when(fill_ref[tile] != 0)
            def _():
                tile_fill(tile).wait()
            return carry

        lax.fori_loop(0, n_tiles, wait_fill, 0)

    def wait_slot(s):
        for _ in range(2):
            pltpu.make_async_copy(hbuf.at[s], xs_ref.at[pl.ds(0, tk)], sem.at[s]).wait()

    @pl.when(i >= 2)
    def _():
        wait_slot(slot)

    hbuf[slot] = _normed(x_ref, g_ref)

    def start(group, carry):
        r0 = pl.multiple_of(group * 8, 8)
        for u in range(8):
            for assignment in range(2):
                dst = pos_ref[assignment * tokens + base + r0 + u]
                pltpu.make_async_copy(hbuf.at[slot, pl.ds(r0 + u, 1)], xs_ref.at[pl.ds(dst, 1)],
                                      sem.at[slot]).start()
        return carry

    lax.fori_loop(0, tk // 8, start, 0)

    @pl.when(i == n_steps - 1)
    def _():
        if n_steps >= 2:
            wait_slot(1 - slot)
        wait_slot(slot)


def _dispatch(x2, gain, pos_flat, fill_flag, n_tiles, tk=512):
    t, d = x2.shape
    return pl.pallas_call(
        functools.partial(_dispatch_body, tk=tk, tokens=t, n_tiles=n_tiles, n_steps=t // tk),
        grid_spec=pltpu.PrefetchScalarGridSpec(
            num_scalar_prefetch=2,
            grid=(t // tk,),
            in_specs=[pl.BlockSpec((tk, d), lambda i, pos, fill: (i, 0)),
                      pl.BlockSpec((1, d), lambda i, pos, fill: (0, 0))],
            out_specs=pl.BlockSpec(memory_space=pl.ANY),
            scratch_shapes=[pltpu.VMEM((2, tk, d), F32),
                            pltpu.VMEM((MOE_TM, d), F32),
                            pltpu.SemaphoreType.DMA((2,)),
                            pltpu.SemaphoreType.DMA(())]),
        out_shape=jax.ShapeDtypeStruct((n_tiles * MOE_TM, d), F32),
        compiler_params=_params(("arbitrary",), 40),
        name="moe_dispatch",
    )(pos_flat, fill_flag, x2, gain.reshape(1, d))


def _combine_body(pos_ref, x_ref, gates_ref, gf_ref, y_ref, o_ref, ybuf, sem,
                  *, tk, tokens, n_steps):
    i = pl.program_id(0)
    slot = i % 2

    def gather(step, s):
        base = step * tk

        def start(r, carry):
            for assignment in range(2):
                src = pos_ref[assignment * tokens + base + r]
                pltpu.make_async_copy(y_ref.at[pl.ds(src, 1)],
                                      ybuf.at[s, assignment, pl.ds(r, 1)], sem.at[s]).start()
            return carry

        lax.fori_loop(0, tk, start, 0, unroll=8)

    @pl.when(i == 0)
    def _():
        gather(0, 0)

    @pl.when(i + 1 < n_steps)
    def _():
        gather(i + 1, 1 - slot)

    for assignment in range(2):
        pltpu.make_async_copy(y_ref.at[pl.ds(0, tk)], ybuf.at[slot, assignment],
                              sem.at[slot]).wait()

    gates = gates_ref[...]
    x = x_ref[...] + gates[:, 0:1] * ybuf[slot, 0] + gates[:, 1:2] * ybuf[slot, 1]
    ms = jnp.mean(x * x, axis=-1, keepdims=True)
    o_ref[...] = x * lax.rsqrt(ms + EPS) * gf_ref[...]


def _combine(x2, y, pos_flat, gates, g_final, tk=512):
    t, d = x2.shape
    return pl.pallas_call(
        functools.partial(_combine_body, tk=tk, tokens=t, n_steps=t // tk),
        grid_spec=pltpu.PrefetchScalarGridSpec(
            num_scalar_prefetch=1,
            grid=(t // tk,),
            in_specs=[pl.BlockSpec((tk, d), lambda i, pos: (i, 0)),
                      pl.BlockSpec((tk, 2), lambda i, pos: (i, 0)),
                      pl.BlockSpec((1, d), lambda i, pos: (0, 0)),
                      pl.BlockSpec(memory_space=pl.ANY)],
            out_specs=pl.BlockSpec((tk, d), lambda i, pos: (i, 0)),
            scratch_shapes=[pltpu.VMEM((2, 2, tk, d), F32),
                            pltpu.SemaphoreType.DMA((2,))]),
        out_shape=jax.ShapeDtypeStruct((t, d), F32),
        compiler_params=_params(("arbitrary",), 48),
        name="moe_combine",
    )(pos_flat, x2, gates, g_final.reshape(1, d), y)


MOE_TM = 512
DENSE_TF = 512
EXPERT_TF = 1024
PROJ_TM = 1024
PROJ_TN = 1024
MOE_GATE_CAST_BLOCKS = 128
MOE_OUT_CAST_BLOCKS = 256


def _routing_tables(route, counts, n_tiles):
    idx1 = route[0].astype(jnp.int32)
    idx2 = route[1].astype(jnp.int32)
    cnt = counts[:, 0].astype(jnp.int32)
    tiles_e = (cnt + MOE_TM - 1) // MOE_TM
    tile_end = jnp.cumsum(tiles_e)
    offset = (tile_end - tiles_e) * MOE_TM
    pos1 = offset[idx1] + route[4].astype(jnp.int32)
    pos2 = offset[idx2] + route[5].astype(jnp.int32)
    n_used = tile_end[-1]
    tiles = jnp.arange(n_tiles, dtype=jnp.int32)
    te = jnp.sum((tiles[:, None] >= tile_end[None, :]).astype(jnp.int32), axis=1)
    te = jnp.minimum(te, N_EXPERTS - 1)
    te = jnp.where(tiles < n_used, te, te[n_used - 1])
    last_of_expert = jnp.any((tiles[:, None] == tile_end[None, :] - 1) & (tiles_e[None, :] > 0),
                             axis=1)
    fill_flag = (last_of_expert | (tiles >= n_used)).astype(jnp.int32)
    live = cnt[te] - (tiles - (tile_end - tiles_e)[te]) * MOE_TM
    tile_rows = jnp.where(tiles < n_used, jnp.clip(live, 0, MOE_TM), 0).astype(jnp.int32)
    pos_flat = jnp.concatenate([pos1, pos2]).astype(jnp.int32)
    gates = jnp.stack([route[2], route[3]], axis=1)
    return pos_flat, gates, te, n_used.reshape(1).astype(jnp.int32), fill_flag, tile_rows


def kernel(x, ln_mix, ln_ffn, ln_final, ret_w_in, ret_w_out, ffn_w_in, ffn_w_out,
           att_w_in, att_w_out, att_rel_bias, moe_router, moe_w_in, moe_w_out):
    b, s, d = x.shape
    t = b * s
    x2 = x.reshape(t, d)
    f_dense = ffn_w_out.shape[1]
    f_expert = moe_w_out.shape[2]

    moe_in2 = moe_w_in[0].reshape(N_EXPERTS * d, 2 * f_expert)
    moe_out2 = moe_w_out[0].reshape(N_EXPERTS * f_expert, d)
    ret_steps = b * RET_HEADS
    att_steps = ATT_GROUPS * b
    cast_moe_gate = _SideCast(moe_in2, MOE_GATE_CAST_BLOCKS, f_expert, 0)
    cast_moe_up = _SideCast(moe_in2, att_steps, f_expert, 1)
    cast_moe_out = _SideCast(moe_out2, MOE_OUT_CAST_BLOCKS, d)
    cast_ffn_gate = _SideCast(ffn_w_in[0], ret_steps, f_dense, 0)
    cast_ffn_up = _SideCast(ffn_w_in[0], ret_steps, f_dense, 1)
    cast_ffn_out = _SideCast(ffn_w_out[0], ret_steps // 2, d)

    h = _rmsnorm(x2, ln_mix[0], BF16)
    proj, (moe_gate_bf,) = _matmul(h, ret_w_in[0], None, BF16, PROJ_TM, PROJ_TN, "ret_proj_in",
                                   sides=[cast_moe_gate])
    pos = jnp.arange(s, dtype=F32)
    inv_freq = ROPE_BASE ** (-jnp.arange(0, RET_QK_DIM, 2, dtype=F32) / RET_QK_DIM)
    ang = pos[:, None] * inv_freq[None, :]
    y, (ffn_gate_bf, ffn_up_bf, ffn_out_bf) = _retention(
        proj.reshape(b, s, -1), jnp.cos(ang), jnp.sin(ang),
        sides=[cast_ffn_gate, cast_ffn_up, cast_ffn_out])
    x2, _ = _matmul(y.reshape(t, RET_HV), ret_w_out[0], x2, F32, PROJ_TM, PROJ_TN // 2,
                    "ret_proj_out")

    dense_tiles = t // MOE_TM
    (x2, h), (moe_out_bf,) = _ffn(
        x2, (ln_ffn[0], ln_mix[1]), ffn_gate_bf[None], ffn_up_bf[None], ffn_out_bf[None],
        jnp.zeros((dense_tiles,), jnp.int32), jnp.full((1,), dense_tiles, jnp.int32),
        jnp.full((dense_tiles,), MOE_TM, jnp.int32),
        MOE_TM, DENSE_TF, 48, "dense_swiglu", sides=[cast_moe_out])

    qkv, _ = _matmul(h, att_w_in[0], None, BF16, PROJ_TM, PROJ_TN, "att_proj_in")
    a, (moe_up_bf,) = _attention(qkv.reshape(b, s, -1), att_rel_bias[0], sides=[cast_moe_up])
    x2, _ = _matmul(a.reshape(t, d), att_w_out[0], x2, F32, PROJ_TM, PROJ_TN, "att_proj_out")

    route, counts = _router(x2, ln_ffn[1], moe_router[0])
    n_tiles = 2 * t // MOE_TM + N_EXPERTS
    pos_flat, gates, te, n_used, fill_flag, tile_rows = _routing_tables(route, counts, n_tiles)
    xs = _dispatch(x2, ln_ffn[1], pos_flat, fill_flag, n_tiles)
    (ys,), _ = _ffn(xs, None, moe_gate_bf.reshape(N_EXPERTS, d, f_expert),
                 moe_up_bf.reshape(N_EXPERTS, d, f_expert),
                 moe_out_bf.reshape(N_EXPERTS, f_expert, d), te, n_used, tile_rows,
                 MOE_TM, EXPERT_TF, 56, "expert_swiglu", half_tiles=True)
    out = _combine(x2, ys, pos_flat, gates, ln_final)
    return out.reshape(b, s, d)
```

```python
import functools
import math
from typing import NamedTuple

import jax
import jax.numpy as jnp
from jax import lax
from jax.experimental import pallas as pl
from jax.experimental.pallas import tpu as pltpu

D_MODEL = 2048
CHUNK = 64
EPS = 1e-6
RET_HEADS = 8
RET_QK_DIM = D_MODEL // RET_HEADS
RET_V_DIM = 2 * RET_QK_DIM
RET_HV = RET_HEADS * RET_V_DIM
ROPE_BASE = 10000.0
ATT_HEADS = 16
ATT_HEAD_DIM = D_MODEL // ATT_HEADS
LEFT_CHUNKS = 8
REL_CLIP = 128
NEG_INF = -1e30
N_EXPERTS = 8

LANES = 128
MIB = 1024 * 1024

F32 = jnp.float32
BF16 = jnp.bfloat16


def _params(semantics, vmem_mib):
    return pltpu.CompilerParams(
        dimension_semantics=semantics, vmem_limit_bytes=vmem_mib * MIB)


def _silu(x):
    return x * (1.0 / (1.0 + jnp.exp(-x)))


class _SideCast(NamedTuple):
    src: jax.Array
    num_blocks: int
    block_cols: int
    col_block: int = 0


def _call_with_sides(body, *, grid, in_specs, args, out_specs, out_shapes, scratch_shapes,
                     sides, params, name, prefetch=()):
    steps = math.prod(grid)
    n_pre, n_in, n_out, n_side = len(prefetch), len(args), len(out_shapes), len(sides)
    in_specs, args = list(in_specs), list(args)
    out_specs, out_shapes = list(out_specs), list(out_shapes)

    def block_spec(side, col_block):
        def index(*ids):
            step = ids[0]
            for extent, pid in zip(grid[1:], ids[1:len(grid)]):
                step = step * extent + pid
            return ((step * side.num_blocks) // steps, col_block)
        return pl.BlockSpec((side.src.shape[0] // side.num_blocks, side.block_cols), index)

    for side in sides:
        assert side.num_blocks <= steps and side.src.shape[0] % side.num_blocks == 0
        in_specs.append(block_spec(side, side.col_block))
        args.append(side.src)
        out_specs.append(block_spec(side, 0))
        out_shapes.append(jax.ShapeDtypeStruct((side.src.shape[0], side.block_cols), BF16))

    def wrapped(*refs):
        pre, refs = refs[:n_pre], refs[n_pre:]
        ins = refs[:n_in]
        srcs = refs[n_in:n_in + n_side]
        outs = refs[n_in + n_side:n_in + n_side + n_out]
        side_outs = refs[n_in + n_side + n_out:n_in + 2 * n_side + n_out]
        scratch = refs[n_in + 2 * n_side + n_out:]
        for src_ref, dst_ref in zip(srcs, side_outs):
            dst_ref[...] = src_ref[...].astype(BF16)
        body(*pre, *ins, *outs, *scratch)

    res = pl.pallas_call(
        wrapped,
        grid_spec=pltpu.PrefetchScalarGridSpec(
            num_scalar_prefetch=n_pre, grid=grid, in_specs=in_specs, out_specs=out_specs,
            scratch_shapes=scratch_shapes),
        out_shape=out_shapes, compiler_params=params, name=name)(*prefetch, *args)
    return res[:n_out], res[n_out:]


def _rmsnorm_body(x_ref, g_ref, o_ref):
    x = x_ref[...]
    ms = jnp.mean(x * x, axis=-1, keepdims=True)
    o_ref[...] = (x * lax.rsqrt(ms + EPS) * g_ref[...]).astype(o_ref.dtype)


def _rmsnorm(x2, g, out_dtype, tm=512):
    t, d = x2.shape
    return pl.pallas_call(
        _rmsnorm_body,
        grid=(t // tm,),
        in_specs=[pl.BlockSpec((tm, d), lambda i: (i, 0)),
                  pl.BlockSpec((1, d), lambda i: (0, 0))],
        out_specs=pl.BlockSpec((tm, d), lambda i: (i, 0)),
        out_shape=jax.ShapeDtypeStruct((t, d), out_dtype),
        compiler_params=_params(("arbitrary",), 32),
        name="rmsnorm",
    )(x2, g.reshape(1, d))


def _matmul_body(*refs, has_res):
    if has_res:
        x_ref, w_ref, r_ref, o_ref, wb_ref = refs
    else:
        x_ref, w_ref, o_ref, wb_ref = refs

    @pl.when(pl.program_id(1) == 0)
    def _():
        wb_ref[...] = w_ref[...].astype(BF16)

    acc = jnp.dot(x_ref[...], wb_ref[...], preferred_element_type=F32)
    if has_res:
        acc = acc + r_ref[...]
    o_ref[...] = acc.astype(o_ref.dtype)


def _matmul(x, w, res, out_dtype, tm, tn, name, sides=()):
    m, k = x.shape
    n = w.shape[1]
    in_specs = [pl.BlockSpec((tm, k), lambda j, i: (i, 0)),
                pl.BlockSpec((k, tn), lambda j, i: (0, j))]
    args = [x, w]
    if res is not None:
        in_specs.append(pl.BlockSpec((tm, tn), lambda j, i: (i, j)))
        args.append(res)
    (out,), side_outs = _call_with_sides(
        functools.partial(_matmul_body, has_res=res is not None),
        grid=(n // tn, m // tm),
        in_specs=in_specs, args=args,
        out_specs=[pl.BlockSpec((tm, tn), lambda j, i: (i, j))],
        out_shapes=[jax.ShapeDtypeStruct((m, n), out_dtype)],
        scratch_shapes=[pltpu.VMEM((k, tn), BF16)],
        sides=sides,
        params=_params(("arbitrary", "arbitrary"), 52),
        name=name)
    return out, side_outs


def _retention_body(q_ref, k_ref, v_ref, g_ref, cos_ref, sin_ref, o_ref, state_ref,
                    *, block_rows, seq):
    L = block_rows
    half = RET_QK_DIM // 2
    head = pl.program_id(1)

    def log_gamma(rows):
        headf = (jnp.zeros((rows, 1), jnp.int32) + head).astype(F32)
        return jnp.log(1.0 - jnp.exp2(-5.0 - headf))

    log_g = log_gamma(L)
    n_i = lax.broadcasted_iota(jnp.int32, (L, L), 0)
    m_i = lax.broadcasted_iota(jnp.int32, (L, L), 1)
    dist = jnp.abs(n_i - m_i).astype(F32)
    visible = (m_i >> 6) <= (n_i >> 6)
    decay = jnp.where(visible, jnp.exp(log_g * dist), 0.0)
    r = lax.broadcasted_iota(jnp.int32, (L, 1), 0).astype(F32)
    q_decay = jnp.exp(log_g * (r + 1.0))
    k_decay = jnp.exp(log_g * (L - 1.0 - r))
    block_decay = jnp.exp(log_gamma(RET_QK_DIM) * float(L))
    scale = RET_QK_DIM ** -0.5

    state_ref[...] = jnp.zeros_like(state_ref)

    def body(s, carry):
        r0 = pl.multiple_of(s * L, L)
        cos = cos_ref[pl.ds(r0, L), :]
        sin = sin_ref[pl.ds(r0, L), :]

        def rot(t):
            t1 = t[:, :half]
            t2 = t[:, half:]
            return jnp.concatenate([t1 * cos - t2 * sin, t1 * sin + t2 * cos], axis=1)

        q = rot(q_ref[pl.ds(r0, L), :].astype(F32)).astype(BF16)
        kf = rot(k_ref[pl.ds(r0, L), :].astype(F32)) * scale
        v = v_ref[pl.ds(r0, L), :]
        s_qk = lax.dot_general(q, kf.astype(BF16), (((1,), (1,)), ((), ())),
                               preferred_element_type=F32) * decay
        inner = jnp.dot(s_qk.astype(BF16), v, preferred_element_type=F32)
        st = state_ref[...]
        cross = jnp.dot(q, st.astype(BF16), preferred_element_type=F32) * q_decay
        kd_t = (kf * k_decay).T.astype(BF16)
        state_ref[...] = st * block_decay + jnp.dot(kd_t, v, preferred_element_type=F32)
        o = inner + cross
        o = o * lax.rsqrt(jnp.mean(o * o, axis=-1, keepdims=True) + EPS)
        gate = g_ref[pl.ds(r0, L), :].astype(F32)
        o_ref[pl.ds(r0, L), :] = (_silu(gate) * o).astype(o_ref.dtype)
        return carry

    lax.fori_loop(0, seq // L, body, 0, unroll=2)


def _retention(proj, cos, sin, sides=(), block_rows=256):
    b, s, _ = proj.shape
    nq = D_MODEL // RET_QK_DIM
    nv = RET_HV // RET_V_DIM
    (out,), side_outs = _call_with_sides(
        functools.partial(_retention_body, block_rows=block_rows, seq=s),
        grid=(b, RET_HEADS),
        in_specs=[
            pl.BlockSpec((None, s, RET_QK_DIM), lambda i, h: (i, 0, h)),
            pl.BlockSpec((None, s, RET_QK_DIM), lambda i, h: (i, 0, nq + h)),
            pl.BlockSpec((None, s, RET_V_DIM), lambda i, h: (i, 0, nv + h)),
            pl.BlockSpec((None, s, RET_V_DIM), lambda i, h: (i, 0, 2 * nv + h)),
            pl.BlockSpec((s, RET_QK_DIM // 2), lambda i, h: (0, 0)),
            pl.BlockSpec((s, RET_QK_DIM // 2), lambda i, h: (0, 0)),
        ],
        args=[proj, proj, proj, proj, cos, sin],
        out_specs=[pl.BlockSpec((None, s, RET_V_DIM), lambda i, h: (i, 0, h))],
        out_shapes=[jax.ShapeDtypeStruct((b, s, RET_HV), BF16)],
        scratch_shapes=[pltpu.VMEM((RET_QK_DIM, RET_V_DIM), F32)],
        sides=sides,
        params=_params(("arbitrary", "arbitrary"), 48),
        name="retention_core")
    return out, side_outs


ATT_PAD = LEFT_CHUNKS * CHUNK
ATT_LQ = 256
ATT_WIN = ATT_LQ + ATT_PAD
ATT_ROLL_W = ATT_LQ + ATT_WIN
REL_PAD = 384


ATT_HPS = 2
ATT_GROUPS = ATT_HEADS // ATT_HPS


def _attention_body(q_ref, k_ref, v_ref, rb_ref, o_ref, bm_ref, *, seq):
    @pl.when(pl.program_id(1) == 0)
    def _():
        t = lax.broadcasted_iota(jnp.int32, (REL_PAD, ATT_ROLL_W), 1)
        j = lax.broadcasted_iota(jnp.int32, (REL_PAD, ATT_ROLL_W), 0)
        idx = jnp.where(t <= ATT_WIN, jnp.clip(ATT_PAD - t, -REL_CLIP, REL_CLIP),
                        REL_CLIP) + REL_CLIP
        onehot = jnp.where(idx == j, 1.0, 0.0).astype(BF16)
        row = lax.broadcasted_iota(jnp.int32, (ATT_LQ, ATT_ROLL_W), 0)
        qc = lax.broadcasted_iota(jnp.int32, (ATT_LQ, ATT_WIN), 0) >> 6
        kc = lax.broadcasted_iota(jnp.int32, (ATT_LQ, ATT_WIN), 1) >> 6
        ahead = kc - qc
        for hh in range(ATT_HPS):
            rb = rb_ref[hh]
            hi = rb.astype(BF16)
            rest = rb - hi.astype(F32)
            mid = rest.astype(BF16)
            lo = (rest - mid.astype(F32)).astype(BF16)
            g = (jnp.dot(hi, onehot, preferred_element_type=F32)
                 + jnp.dot(mid, onehot, preferred_element_type=F32)
                 + jnp.dot(lo, onehot, preferred_element_type=F32))
            rows = jnp.broadcast_to(g[0:1, :], (ATT_LQ, ATT_ROLL_W))
            shift = 1
            while shift < ATT_LQ:
                rows = jnp.where((row & shift) != 0, pltpu.roll(rows, shift, 1), rows)
                shift *= 2
            banded = jnp.where(ahead <= LEFT_CHUNKS, rows[:, :ATT_WIN], NEG_INF)
            bm_ref[hh] = jnp.where(ahead >= 0, banded, NEG_INF)

    scale = ATT_HEAD_DIM ** -0.5
    for q0 in range(0, seq, ATT_LQ):
        k_lo = max(0, q0 - ATT_PAD)
        width = q0 + ATT_LQ - k_lo
        for hh in range(ATT_HPS):
            cols = slice(hh * ATT_HEAD_DIM, (hh + 1) * ATT_HEAD_DIM)
            q = q_ref[q0:q0 + ATT_LQ, cols]
            kw = k_ref[k_lo:k_lo + width, cols]
            vw = v_ref[k_lo:k_lo + width, cols]
            logits = lax.dot_general(q, kw, (((1,), (1,)), ((), ())),
                                     preferred_element_type=F32) * scale
            logits = logits + bm_ref[hh, :, ATT_WIN - width:]
            mx = jnp.max(logits, axis=-1, keepdims=True)
            p = jnp.exp(logits - mx)
            denom = jnp.sum(p, axis=-1, keepdims=True)
            o = jnp.dot(p.astype(BF16), vw, preferred_element_type=F32) / denom
            o_ref[q0:q0 + ATT_LQ, cols] = o.astype(o_ref.dtype)


def _attention(qkv, rel_bias, sides=()):
    b, s, _ = qkv.shape
    rb = jnp.zeros((ATT_HEADS, 8, REL_PAD), F32).at[:, 0, :2 * REL_CLIP + 1].set(rel_bias)
    wide = ATT_HPS * ATT_HEAD_DIM
    (out,), side_outs = _call_with_sides(
        functools.partial(_attention_body, seq=s),
        grid=(ATT_GROUPS, b),
        in_specs=[
            pl.BlockSpec((None, s, wide), lambda h, i: (i, 0, h)),
            pl.BlockSpec((None, s, wide), lambda h, i: (i, 0, ATT_GROUPS + h)),
            pl.BlockSpec((None, s, wide), lambda h, i: (i, 0, 2 * ATT_GROUPS + h)),
            pl.BlockSpec((ATT_HPS, 8, REL_PAD), lambda h, i: (h, 0, 0)),
        ],
        args=[qkv, qkv, qkv, rb],
        out_specs=[pl.BlockSpec((None, s, wide), lambda h, i: (i, 0, h))],
        out_shapes=[jax.ShapeDtypeStruct((b, s, D_MODEL), BF16)],
        scratch_shapes=[pltpu.VMEM((ATT_HPS, ATT_LQ, ATT_WIN), F32)],
        sides=sides,
        params=_params(("arbitrary", "arbitrary"), 48),
        name="chunk_attention")
    return out, side_outs


def _ffn_body(te_ref, nu_ref, tr_ref, *refs, normalize, partial_tiles):
    del te_ref, nu_ref
    if normalize:
        x_ref, g_ref, g_next_ref, wg_ref, wu_ref, wo_ref, o_ref, h_next_ref, xb_ref = refs
    else:
        x_ref, wg_ref, wu_ref, wo_ref, o_ref, xb_ref = refs
    tm = xb_ref.shape[0]

    @pl.when(pl.program_id(1) == 0)
    def _():
        x = x_ref[...]
        if normalize:
            ms = jnp.mean(x * x, axis=-1, keepdims=True)
            xb_ref[...] = (x * lax.rsqrt(ms + EPS) * g_ref[...]).astype(BF16)
            o_ref[...] = x
        else:
            xb_ref[...] = x.astype(BF16)
            o_ref[...] = jnp.zeros_like(o_ref)

    def accumulate(m):
        xb = xb_ref[:m, :]
        gate = jnp.dot(xb, wg_ref[...], preferred_element_type=F32)
        up = jnp.dot(xb, wu_ref[...], preferred_element_type=F32)
        act = (_silu(gate) * up).astype(BF16)
        o_ref[:m, :] += jnp.dot(act, wo_ref[...], preferred_element_type=F32)

    rows_in_tile = tr_ref[pl.program_id(0)]
    if partial_tiles:
        quarter = tm // 4
        for k in range(1, 5):
            lo, hi = (k - 1) * quarter, k * quarter

            @pl.when(jnp.logical_and(rows_in_tile > lo, rows_in_tile <= hi))
            def _(hi=hi):
                accumulate(hi)
    else:
        @pl.when(rows_in_tile > 0)
        def _():
            accumulate(tm)

    if normalize:
        @pl.when(pl.program_id(1) == pl.num_programs(1) - 1)
        def _():
            o = o_ref[...]
            ms = jnp.mean(o * o, axis=-1, keepdims=True)
            h_next_ref[...] = (o * lax.rsqrt(ms + EPS) * g_next_ref[...]).astype(BF16)


def _ffn(x, gains, w_gate, w_up, w_out, tile_expert, n_used, tile_rows, tm, tf, vmem_mib, name,
         partial_tiles=False, sides=()):
    rows, d = x.shape
    f = w_out.shape[1]
    nj = f // tf
    nt = rows // tm
    normalize = gains is not None

    def col(i, j, nu):
        return jnp.where(i < nu[0], j, nj - 1)

    def row_block():
        return pl.BlockSpec((tm, d), lambda i, j, te, nu, tr: (i, 0))

    in_specs = [pl.BlockSpec(
        (tm, d), lambda i, j, te, nu, tr: (jnp.maximum(jnp.minimum(i, nu[0] - 1), 0), 0))]
    args = [x]
    out_specs = [row_block()]
    out_shapes = [jax.ShapeDtypeStruct((rows, d), F32)]
    if normalize:
        for gain in gains:
            in_specs.append(pl.BlockSpec((1, d), lambda i, j, te, nu, tr: (0, 0)))
            args.append(gain.reshape(1, d))
        out_specs.append(row_block())
        out_shapes.append(jax.ShapeDtypeStruct((rows, d), BF16))
    in_specs += [
        pl.BlockSpec((None, d, tf), lambda i, j, te, nu, tr: (te[i], 0, col(i, j, nu))),
        pl.BlockSpec((None, d, tf), lambda i, j, te, nu, tr: (te[i], 0, col(i, j, nu))),
        pl.BlockSpec((None, tf, d), lambda i, j, te, nu, tr: (te[i], col(i, j, nu), 0)),
    ]
    args += [w_gate, w_up, w_out]
    return _call_with_sides(
        functools.partial(_ffn_body, normalize=normalize, partial_tiles=partial_tiles),
        grid=(nt, nj),
        in_specs=in_specs, args=args,
        out_specs=out_specs, out_shapes=out_shapes,
        scratch_shapes=[pltpu.VMEM((tm, d), BF16)],
        sides=sides,
        params=_params(("arbitrary", "arbitrary"), vmem_mib),
        name=name,
        prefetch=(tile_expert, n_used, tile_rows))


def _normed(x_ref, g_ref):
    x = x_ref[...]
    ms = jnp.mean(x * x, axis=-1, keepdims=True)
    return x * lax.rsqrt(ms + EPS) * g_ref[...]


def _router_body(x_ref, g_ref, wr_ref, o_ref, cnt_ref, carry_ref, *, tm):
    @pl.when(pl.program_id(0) == 0)
    def _():
        carry_ref[...] = jnp.zeros_like(carry_ref)

    def split(a):
        hi = a.astype(BF16)
        return hi, (a - hi.astype(F32)).astype(BF16)

    h_hi, h_lo = split(_normed(x_ref, g_ref))
    w_hi, w_lo = split(wr_ref[...])
    logits = (jnp.dot(h_hi, w_hi, preferred_element_type=F32)
              + jnp.dot(h_hi, w_lo, preferred_element_type=F32)
              + jnp.dot(h_lo, w_hi, preferred_element_type=F32))
    lt = logits.T[:N_EXPERTS, :]
    e = lax.broadcasted_iota(jnp.int32, (N_EXPERTS, tm), 0)
    m1 = jnp.max(lt, axis=0, keepdims=True)
    i1 = jnp.min(jnp.where(lt == m1, e, N_EXPERTS), axis=0, keepdims=True)
    lt2 = jnp.where(e == i1, -jnp.inf, lt)
    m2 = jnp.max(lt2, axis=0, keepdims=True)
    i2 = jnp.min(jnp.where(lt2 == m2, e, N_EXPERTS), axis=0, keepdims=True)
    ex = jnp.exp(m2 - m1)
    g1 = 1.0 / (1.0 + ex)
    g2 = ex / (1.0 + ex)

    oh1 = jnp.where(e == i1, 1.0, 0.0)
    oh2 = jnp.where(e == i2, 1.0, 0.0)
    before = jnp.where(lax.broadcasted_iota(jnp.int32, (tm, tm), 0)
                       < lax.broadcasted_iota(jnp.int32, (tm, tm), 1), 1.0, 0.0).astype(BF16)
    c1 = jnp.dot(oh1.astype(BF16), before, preferred_element_type=F32)
    c2 = jnp.dot(oh2.astype(BF16), before, preferred_element_type=F32)
    t1 = jnp.sum(oh1, axis=1, keepdims=True)
    t2 = jnp.sum(oh2, axis=1, keepdims=True)
    carry = carry_ref[...]
    base = carry[:, :1]
    r1 = jnp.sum(oh1 * (base + c1), axis=0, keepdims=True)
    r2 = jnp.sum(oh2 * (base + t1 + c2), axis=0, keepdims=True)
    carry = carry + t1 + t2
    carry_ref[...] = carry
    cnt_ref[...] = carry

    out = jnp.where(e == 0, i1.astype(F32),
          jnp.where(e == 1, i2.astype(F32),
          jnp.where(e == 2, g1,
          jnp.where(e == 3, g2,
          jnp.where(e == 4, r1,
          jnp.where(e == 5, r2, 0.0))))))
    o_ref[...] = out


def _router(x2, gain, w_router, tm=512):
    t, d = x2.shape
    wr = jnp.pad(w_router, ((0, 0), (0, LANES - N_EXPERTS)))
    return pl.pallas_call(
        functools.partial(_router_body, tm=tm),
        grid=(t // tm,),
        in_specs=[pl.BlockSpec((tm, d), lambda i: (i, 0)),
                  pl.BlockSpec((1, d), lambda i: (0, 0)),
                  pl.BlockSpec((d, LANES), lambda i: (0, 0))],
        out_specs=[pl.BlockSpec((N_EXPERTS, tm), lambda i: (0, i)),
                   pl.BlockSpec((N_EXPERTS, LANES), lambda i: (0, 0))],
        out_shape=[jax.ShapeDtypeStruct((N_EXPERTS, t), F32),
                   jax.ShapeDtypeStruct((N_EXPERTS, LANES), F32)],
        scratch_shapes=[pltpu.VMEM((N_EXPERTS, LANES), F32)],
        compiler_params=_params(("arbitrary",), 32),
        name="router_top2",
    )(x2, gain.reshape(1, d), wr)


def _dispatch_body(pos_ref, fill_ref, x_ref, g_ref, xs_ref, hbuf, zbuf, sem, zsem,
                   *, tk, tokens, n_tiles, n_steps):
    i = pl.program_id(0)
    slot = i % 2
    base = i * tk

    def tile_fill(tile):
        return pltpu.make_async_copy(zbuf, xs_ref.at[pl.ds(tile * MOE_TM, MOE_TM)], zsem)

    @pl.when(i == 0)
    def _():
        zbuf[...] = jnp.zeros_like(zbuf)

        def start_fill(tile, carry):
            @pl.when(fill_ref[tile] != 0)
            def _():
                tile_fill(tile).start()
            return carry

        lax.fori_loop(0, n_tiles, start_fill, 0)

        def wait_fill(tile, carry):
            @pl.when(fill_ref[tile] != 0)
            def _():
                tile_fill(tile).wait()
            return carry

        lax.fori_loop(0, n_tiles, wait_fill, 0)

    def wait_slot(s):
        for _ in range(2):
            pltpu.make_async_copy(hbuf.at[s], xs_ref.at[pl.ds(0, tk)], sem.at[s]).wait()

    @pl.when(i >= 2)
    def _():
        wait_slot(slot)

    hbuf[slot] = _normed(x_ref, g_ref)

    def start(group, carry):
        r0 = pl.multiple_of(group * 8, 8)
        for u in range(8):
            for assignment in range(2):
                dst = pos_ref[assignment * tokens + base + r0 + u]
                pltpu.make_async_copy(hbuf.at[slot, pl.ds(r0 + u, 1)], xs_ref.at[pl.ds(dst, 1)],
                                      sem.at[slot]).start()
        return carry

    lax.fori_loop(0, tk // 8, start, 0)

    @pl.when(i == n_steps - 1)
    def _():
        if n_steps >= 2:
            wait_slot(1 - slot)
        wait_slot(slot)


def _dispatch(x2, gain, pos_flat, fill_flag, n_tiles, tk=512):
    t, d = x2.shape
    return pl.pallas_call(
        functools.partial(_dispatch_body, tk=tk, tokens=t, n_tiles=n_tiles, n_steps=t // tk),
        grid_spec=pltpu.PrefetchScalarGridSpec(
            num_scalar_prefetch=2,
            grid=(t // tk,),
            in_specs=[pl.BlockSpec((tk, d), lambda i, pos, fill: (i, 0)),
                      pl.BlockSpec((1, d), lambda i, pos, fill: (0, 0))],
            out_specs=pl.BlockSpec(memory_space=pl.ANY),
            scratch_shapes=[pltpu.VMEM((2, tk, d), F32),
                            pltpu.VMEM((MOE_TM, d), F32),
                            pltpu.SemaphoreType.DMA((2,)),
                            pltpu.SemaphoreType.DMA(())]),
        out_shape=jax.ShapeDtypeStruct((n_tiles * MOE_TM, d), F32),
        compiler_params=_params(("arbitrary",), 40),
        name="moe_dispatch",
    )(pos_flat, fill_flag, x2, gain.reshape(1, d))


def _combine_body(pos_ref, x_ref, gates_ref, gf_ref, y_ref, o_ref, ybuf, sem,
                  *, tk, tokens, n_steps):
    i = pl.program_id(0)
    slot = i % 2

    def gather(step, s):
        base = step * tk

        def start(r, carry):
            for assignment in range(2):
                src = pos_ref[assignment * tokens + base + r]
                pltpu.make_async_copy(y_ref.at[pl.ds(src, 1)],
                                      ybuf.at[s, assignment, pl.ds(r, 1)], sem.at[s]).start()
            return carry

        lax.fori_loop(0, tk, start, 0, unroll=8)

    @pl.when(i == 0)
    def _():
        gather(0, 0)

    @pl.when(i + 1 < n_steps)
    def _():
        gather(i + 1, 1 - slot)

    for assignment in range(2):
        pltpu.make_async_copy(y_ref.at[pl.ds(0, tk)], ybuf.at[slot, assignment],
                              sem.at[slot]).wait()

    gates = gates_ref[...]
    x = x_ref[...] + gates[:, 0:1] * ybuf[slot, 0] + gates[:, 1:2] * ybuf[slot, 1]
    ms = jnp.mean(x * x, axis=-1, keepdims=True)
    o_ref[...] = x * lax.rsqrt(ms + EPS) * gf_ref[...]


def _combine(x2, y, pos_flat, gates, g_final, tk=256):
    t, d = x2.shape
    return pl.pallas_call(
        functools.partial(_combine_body, tk=tk, tokens=t, n_steps=t // tk),
        grid_spec=pltpu.PrefetchScalarGridSpec(
            num_scalar_prefetch=1,
            grid=(t // tk,),
            in_specs=[pl.BlockSpec((tk, d), lambda i, pos: (i, 0)),
                      pl.BlockSpec((tk, 2), lambda i, pos: (i, 0)),
                      pl.BlockSpec((1, d), lambda i, pos: (0, 0)),
                      pl.BlockSpec(memory_space=pl.ANY)],
            out_specs=pl.BlockSpec((tk, d), lambda i, pos: (i, 0)),
            scratch_shapes=[pltpu.VMEM((2, 2, tk, d), F32),
                            pltpu.SemaphoreType.DMA((2,))]),
        out_shape=jax.ShapeDtypeStruct((t, d), F32),
        compiler_params=_params(("arbitrary",), 48),
        name="moe_combine",
    )(pos_flat, x2, gates, g_final.reshape(1, d), y)


MOE_TM = 512
DENSE_TF = 512
EXPERT_TF = 1024
PROJ_TM = 1024
PROJ_TN = 1024
MOE_GATE_CAST_BLOCKS = 128
MOE_OUT_CAST_BLOCKS = 256


def _routing_tables(route, counts, n_tiles):
    idx1 = route[0].astype(jnp.int32)
    idx2 = route[1].astype(jnp.int32)
    cnt = counts[:, 0].astype(jnp.int32)
    tiles_e = (cnt + MOE_TM - 1) // MOE_TM
    tile_end = jnp.cumsum(tiles_e)
    offset = (tile_end - tiles_e) * MOE_TM
    pos1 = offset[idx1] + route[4].astype(jnp.int32)
    pos2 = offset[idx2] + route[5].astype(jnp.int32)
    n_used = tile_end[-1]
    tiles = jnp.arange(n_tiles, dtype=jnp.int32)
    te = jnp.sum((tiles[:, None] >= tile_end[None, :]).astype(jnp.int32), axis=1)
    te = jnp.minimum(te, N_EXPERTS - 1)
    te = jnp.where(tiles < n_used, te, te[n_used - 1])
    last_of_expert = jnp.any((tiles[:, None] == tile_end[None, :] - 1) & (tiles_e[None, :] > 0),
                             axis=1)
    fill_flag = (last_of_expert | (tiles >= n_used)).astype(jnp.int32)
    live = cnt[te] - (tiles - (tile_end - tiles_e)[te]) * MOE_TM
    tile_rows = jnp.where(tiles < n_used, jnp.clip(live, 0, MOE_TM), 0).astype(jnp.int32)
    pos_flat = jnp.concatenate([pos1, pos2]).astype(jnp.int32)
    gates = jnp.stack([route[2], route[3]], axis=1)
    return pos_flat, gates, te, n_used.reshape(1).astype(jnp.int32), fill_flag, tile_rows


def kernel(x, ln_mix, ln_ffn, ln_final, ret_w_in, ret_w_out, ffn_w_in, ffn_w_out,
           att_w_in, att_w_out, att_rel_bias, moe_router, moe_w_in, moe_w_out):
    b, s, d = x.shape
    t = b * s
    x2 = x.reshape(t, d)
    f_dense = ffn_w_out.shape[1]
    f_expert = moe_w_out.shape[2]

    moe_in2 = moe_w_in[0].reshape(N_EXPERTS * d, 2 * f_expert)
    moe_out2 = moe_w_out[0].reshape(N_EXPERTS * f_expert, d)
    ret_steps = b * RET_HEADS
    att_steps = ATT_GROUPS * b
    cast_moe_gate = _SideCast(moe_in2, MOE_GATE_CAST_BLOCKS, f_expert, 0)
    cast_moe_up = _SideCast(moe_in2, att_steps, f_expert, 1)
    cast_moe_out = _SideCast(moe_out2, MOE_OUT_CAST_BLOCKS, d)
    cast_ffn_gate = _SideCast(ffn_w_in[0], ret_steps, f_dense, 0)
    cast_ffn_up = _SideCast(ffn_w_in[0], ret_steps, f_dense, 1)
    cast_ffn_out = _SideCast(ffn_w_out[0], ret_steps // 2, d)

    h = _rmsnorm(x2, ln_mix[0], BF16)
    proj, (moe_gate_bf,) = _matmul(h, ret_w_in[0], None, BF16, PROJ_TM, PROJ_TN, "ret_proj_in",
                                   sides=[cast_moe_gate])
    pos = jnp.arange(s, dtype=F32)
    inv_freq = ROPE_BASE ** (-jnp.arange(0, RET_QK_DIM, 2, dtype=F32) / RET_QK_DIM)
    ang = pos[:, None] * inv_freq[None, :]
    y, (ffn_gate_bf, ffn_up_bf, ffn_out_bf) = _retention(
        proj.reshape(b, s, -1), jnp.cos(ang), jnp.sin(ang),
        sides=[cast_ffn_gate, cast_ffn_up, cast_ffn_out])
    x2, _ = _matmul(y.reshape(t, RET_HV), ret_w_out[0], x2, F32, PROJ_TM, PROJ_TN // 2,
                    "ret_proj_out")

    dense_tiles = t // MOE_TM
    (x2, h), (moe_out_bf,) = _ffn(
        x2, (ln_ffn[0], ln_mix[1]), ffn_gate_bf[None], ffn_up_bf[None], ffn_out_bf[None],
        jnp.zeros((dense_tiles,), jnp.int32), jnp.full((1,), dense_tiles, jnp.int32),
        jnp.full((dense_tiles,), MOE_TM, jnp.int32),
        MOE_TM, DENSE_TF, 48, "dense_swiglu", sides=[cast_moe_out])

    qkv, _ = _matmul(h, att_w_in[0], None, BF16, PROJ_TM, PROJ_TN, "att_proj_in")
    a, (moe_up_bf,) = _attention(qkv.reshape(b, s, -1), att_rel_bias[0], sides=[cast_moe_up])
    x2, _ = _matmul(a.reshape(t, d), att_w_out[0], x2, F32, PROJ_TM, PROJ_TN, "att_proj_out")

    route, counts = _router(x2, ln_ffn[1], moe_router[0])
    n_tiles = 2 * t // MOE_TM + N_EXPERTS
    pos_flat, gates, te, n_used, fill_flag, tile_rows = _routing_tables(route, counts, n_tiles)
    xs = _dispatch(x2, ln_ffn[1], pos_flat, fill_flag, n_tiles)
    (ys,), _ = _ffn(xs, None, moe_gate_bf.reshape(N_EXPERTS, d, f_expert),
                 moe_up_bf.reshape(N_EXPERTS, d, f_expert),
                 moe_out_bf.reshape(N_EXPERTS, f_expert, d), te, n_used, tile_rows,
                 MOE_TM, EXPERT_TF, 56, "expert_swiglu", partial_tiles=True)
    out = _combine(x2, ys, pos_flat, gates, ln_final)
    return out.reshape(b, s, d)
```

```python
import functools
import math
from typing import NamedTuple

import jax
import jax.numpy as jnp
from jax import lax
from jax.experimental import pallas as pl
from jax.experimental.pallas import tpu as pltpu

D_MODEL = 2048
CHUNK = 64
EPS = 1e-6
RET_HEADS = 8
RET_QK_DIM = D_MODEL // RET_HEADS
RET_V_DIM = 2 * RET_QK_DIM
RET_HV = RET_HEADS * RET_V_DIM
ROPE_BASE = 10000.0
ATT_HEADS = 16
ATT_HEAD_DIM = D_MODEL // ATT_HEADS
LEFT_CHUNKS = 8
REL_CLIP = 128
NEG_INF = -1e30
N_EXPERTS = 8

LANES = 128
MIB = 1024 * 1024

F32 = jnp.float32
BF16 = jnp.bfloat16


def _params(semantics, vmem_mib):
    return pltpu.CompilerParams(
        dimension_semantics=semantics, vmem_limit_bytes=vmem_mib * MIB)


def _silu(x):
    return x * (1.0 / (1.0 + jnp.exp(-x)))


class _SideCast(NamedTuple):
    src: jax.Array
    num_blocks: int
    block_cols: int
    col_block: int = 0


def _call_with_sides(body, *, grid, in_specs, args, out_specs, out_shapes, scratch_shapes,
                     sides, params, name, prefetch=()):
    steps = math.prod(grid)
    n_pre, n_in, n_out, n_side = len(prefetch), len(args), len(out_shapes), len(sides)
    in_specs, args = list(in_specs), list(args)
    out_specs, out_shapes = list(out_specs), list(out_shapes)

    def block_spec(side, col_block):
        def index(*ids):
            step = ids[0]
            for extent, pid in zip(grid[1:], ids[1:len(grid)]):
                step = step * extent + pid
            return ((step * side.num_blocks) // steps, col_block)
        return pl.BlockSpec((side.src.shape[0] // side.num_blocks, side.block_cols), index)

    for side in sides:
        assert side.num_blocks <= steps and side.src.shape[0] % side.num_blocks == 0
        in_specs.append(block_spec(side, side.col_block))
        args.append(side.src)
        out_specs.append(block_spec(side, 0))
        out_shapes.append(jax.ShapeDtypeStruct((side.src.shape[0], side.block_cols), BF16))

    def wrapped(*refs):
        pre, refs = refs[:n_pre], refs[n_pre:]
        ins = refs[:n_in]
        srcs = refs[n_in:n_in + n_side]
        outs = refs[n_in + n_side:n_in + n_side + n_out]
        side_outs = refs[n_in + n_side + n_out:n_in + 2 * n_side + n_out]
        scratch = refs[n_in + 2 * n_side + n_out:]
        for src_ref, dst_ref in zip(srcs, side_outs):
            dst_ref[...] = src_ref[...].astype(BF16)
        body(*pre, *ins, *outs, *scratch)

    res = pl.pallas_call(
        wrapped,
        grid_spec=pltpu.PrefetchScalarGridSpec(
            num_scalar_prefetch=n_pre, grid=grid, in_specs=in_specs, out_specs=out_specs,
            scratch_shapes=scratch_shapes),
        out_shape=out_shapes, compiler_params=params, name=name)(*prefetch, *args)
    return res[:n_out], res[n_out:]


def _rmsnorm_body(x_ref, g_ref, o_ref):
    x = x_ref[...]
    ms = jnp.mean(x * x, axis=-1, keepdims=True)
    o_ref[...] = (x * lax.rsqrt(ms + EPS) * g_ref[...]).astype(o_ref.dtype)


def _rmsnorm(x2, g, out_dtype, tm=512):
    t, d = x2.shape
    return pl.pallas_call(
        _rmsnorm_body,
        grid=(t // tm,),
        in_specs=[pl.BlockSpec((tm, d), lambda i: (i, 0)),
                  pl.BlockSpec((1, d), lambda i: (0, 0))],
        out_specs=pl.BlockSpec((tm, d), lambda i: (i, 0)),
        out_shape=jax.ShapeDtypeStruct((t, d), out_dtype),
        compiler_params=_params(("arbitrary",), 32),
        name="rmsnorm",
    )(x2, g.reshape(1, d))


def _matmul_body(*refs, has_res):
    if has_res:
        x_ref, w_ref, r_ref, o_ref, wb_ref = refs
    else:
        x_ref, w_ref, o_ref, wb_ref = refs

    @pl.when(pl.program_id(1) == 0)
    def _():
        wb_ref[...] = w_ref[...].astype(BF16)

    acc = jnp.dot(x_ref[...], wb_ref[...], preferred_element_type=F32)
    if has_res:
        acc = acc + r_ref[...]
    o_ref[...] = acc.astype(o_ref.dtype)


def _matmul(x, w, res, out_dtype, tm, tn, name, sides=()):
    m, k = x.shape
    n = w.shape[1]
    in_specs = [pl.BlockSpec((tm, k), lambda j, i: (i, 0)),
                pl.BlockSpec((k, tn), lambda j, i: (0, j))]
    args = [x, w]
    if res is not None:
        in_specs.append(pl.BlockSpec((tm, tn), lambda j, i: (i, j)))
        args.append(res)
    (out,), side_outs = _call_with_sides(
        functools.partial(_matmul_body, has_res=res is not None),
        grid=(n // tn, m // tm),
        in_specs=in_specs, args=args,
        out_specs=[pl.BlockSpec((tm, tn), lambda j, i: (i, j))],
        out_shapes=[jax.ShapeDtypeStruct((m, n), out_dtype)],
        scratch_shapes=[pltpu.VMEM((k, tn), BF16)],
        sides=sides,
        params=_params(("arbitrary", "arbitrary"), 52),
        name=name)
    return out, side_outs


def _retention_body(q_ref, k_ref, v_ref, g_ref, cos_ref, sin_ref, o_ref, state_ref,
                    *, block_rows, seq):
    L = block_rows
    half = RET_QK_DIM // 2
    head = pl.program_id(1)

    def log_gamma(rows):
        headf = (jnp.zeros((rows, 1), jnp.int32) + head).astype(F32)
        return jnp.log(1.0 - jnp.exp2(-5.0 - headf))

    log_g = log_gamma(L)
    n_i = lax.broadcasted_iota(jnp.int32, (L, L), 0)
    m_i = lax.broadcasted_iota(jnp.int32, (L, L), 1)
    dist = jnp.abs(n_i - m_i).astype(F32)
    visible = (m_i >> 6) <= (n_i >> 6)
    decay = jnp.where(visible, jnp.exp(log_g * dist), 0.0)
    r = lax.broadcasted_iota(jnp.int32, (L, 1), 0).astype(F32)
    q_decay = jnp.exp(log_g * (r + 1.0))
    k_decay = jnp.exp(log_g * (L - 1.0 - r))
    block_decay = jnp.exp(log_gamma(RET_QK_DIM) * float(L))
    scale = RET_QK_DIM ** -0.5

    state_ref[...] = jnp.zeros_like(state_ref)

    def body(s, carry):
        r0 = pl.multiple_of(s * L, L)
        cos = cos_ref[pl.ds(r0, L), :]
        sin = sin_ref[pl.ds(r0, L), :]

        def rot(t):
            t1 = t[:, :half]
            t2 = t[:, half:]
            return jnp.concatenate([t1 * cos - t2 * sin, t1 * sin + t2 * cos], axis=1)

        q = rot(q_ref[pl.ds(r0, L), :].astype(F32)).astype(BF16)
        kf = rot(k_ref[pl.ds(r0, L), :].astype(F32)) * scale
        v = v_ref[pl.ds(r0, L), :]
        s_qk = lax.dot_general(q, kf.astype(BF16), (((1,), (1,)), ((), ())),
                               preferred_element_type=F32) * decay
        inner = jnp.dot(s_qk.astype(BF16), v, preferred_element_type=F32)
        st = state_ref[...]
        cross = jnp.dot(q, st.astype(BF16), preferred_element_type=F32) * q_decay
        kd_t = (kf * k_decay).T.astype(BF16)
        state_ref[...] = st * block_decay + jnp.dot(kd_t, v, preferred_element_type=F32)
        o = inner + cross
        o = o * lax.rsqrt(jnp.mean(o * o, axis=-1, keepdims=True) + EPS)
        gate = g_ref[pl.ds(r0, L), :].astype(F32)
        o_ref[pl.ds(r0, L), :] = (_silu(gate) * o).astype(o_ref.dtype)
        return carry

    lax.fori_loop(0, seq // L, body, 0, unroll=2)


def _retention(proj, cos, sin, sides=(), block_rows=256):
    b, s, _ = proj.shape
    nq = D_MODEL // RET_QK_DIM
    nv = RET_HV // RET_V_DIM
    (out,), side_outs = _call_with_sides(
        functools.partial(_retention_body, block_rows=block_rows, seq=s),
        grid=(b, RET_HEADS),
        in_specs=[
            pl.BlockSpec((None, s, RET_QK_DIM), lambda i, h: (i, 0, h)),
            pl.BlockSpec((None, s, RET_QK_DIM), lambda i, h: (i, 0, nq + h)),
            pl.BlockSpec((None, s, RET_V_DIM), lambda i, h: (i, 0, nv + h)),
            pl.BlockSpec((None, s, RET_V_DIM), lambda i, h: (i, 0, 2 * nv + h)),
            pl.BlockSpec((s, RET_QK_DIM // 2), lambda i, h: (0, 0)),
            pl.BlockSpec((s, RET_QK_DIM // 2), lambda i, h: (0, 0)),
        ],
        args=[proj, proj, proj, proj, cos, sin],
        out_specs=[pl.BlockSpec((None, s, RET_V_DIM), lambda i, h: (i, 0, h))],
        out_shapes=[jax.ShapeDtypeStruct((b, s, RET_HV), BF16)],
        scratch_shapes=[pltpu.VMEM((RET_QK_DIM, RET_V_DIM), F32)],
        sides=sides,
        params=_params(("arbitrary", "arbitrary"), 48),
        name="retention_core")
    return out, side_outs


ATT_PAD = LEFT_CHUNKS * CHUNK
ATT_LQ = 256
ATT_WIN = ATT_LQ + ATT_PAD
ATT_ROLL_W = ATT_LQ + ATT_WIN
REL_PAD = 384


ATT_HPS = 2
ATT_GROUPS = ATT_HEADS // ATT_HPS


def _attention_body(q_ref, k_ref, v_ref, rb_ref, o_ref, bm_ref, *, seq):
    @pl.when(pl.program_id(1) == 0)
    def _():
        t = lax.broadcasted_iota(jnp.int32, (REL_PAD, ATT_ROLL_W), 1)
        j = lax.broadcasted_iota(jnp.int32, (REL_PAD, ATT_ROLL_W), 0)
        idx = jnp.where(t <= ATT_WIN, jnp.clip(ATT_PAD - t, -REL_CLIP, REL_CLIP),
                        REL_CLIP) + REL_CLIP
        onehot = jnp.where(idx == j, 1.0, 0.0).astype(BF16)
        row = lax.broadcasted_iota(jnp.int32, (ATT_LQ, ATT_ROLL_W), 0)
        qc = lax.broadcasted_iota(jnp.int32, (ATT_LQ, ATT_WIN), 0) >> 6
        kc = lax.broadcasted_iota(jnp.int32, (ATT_LQ, ATT_WIN), 1) >> 6
        ahead = kc - qc
        for hh in range(ATT_HPS):
            rb = rb_ref[hh]
            hi = rb.astype(BF16)
            rest = rb - hi.astype(F32)
            mid = rest.astype(BF16)
            lo = (rest - mid.astype(F32)).astype(BF16)
            g = (jnp.dot(hi, onehot, preferred_element_type=F32)
                 + jnp.dot(mid, onehot, preferred_element_type=F32)
                 + jnp.dot(lo, onehot, preferred_element_type=F32))
            rows = jnp.broadcast_to(g[0:1, :], (ATT_LQ, ATT_ROLL_W))
            shift = 1
            while shift < ATT_LQ:
                rows = jnp.where((row & shift) != 0, pltpu.roll(rows, shift, 1), rows)
                shift *= 2
            banded = jnp.where(ahead <= LEFT_CHUNKS, rows[:, :ATT_WIN], NEG_INF)
            bm_ref[hh] = jnp.where(ahead >= 0, banded, NEG_INF)

    scale = ATT_HEAD_DIM ** -0.5
    for q0 in range(0, seq, ATT_LQ):
        k_lo = max(0, q0 - ATT_PAD)
        width = q0 + ATT_LQ - k_lo
        for hh in range(ATT_HPS):
            cols = slice(hh * ATT_HEAD_DIM, (hh + 1) * ATT_HEAD_DIM)
            q = q_ref[q0:q0 + ATT_LQ, cols]
            kw = k_ref[k_lo:k_lo + width, cols]
            vw = v_ref[k_lo:k_lo + width, cols]
            logits = lax.dot_general(q, kw, (((1,), (1,)), ((), ())),
                                     preferred_element_type=F32) * scale
            logits = logits + bm_ref[hh, :, ATT_WIN - width:]
            mx = jnp.max(logits, axis=-1, keepdims=True)
            p = jnp.exp(logits - mx)
            denom = jnp.sum(p, axis=-1, keepdims=True)
            o = jnp.dot(p.astype(BF16), vw, preferred_element_type=F32) / denom
            o_ref[q0:q0 + ATT_LQ, cols] = o.astype(o_ref.dtype)


def _attention(qkv, rel_bias, sides=()):
    b, s, _ = qkv.shape
    rb = jnp.zeros((ATT_HEADS, 8, REL_PAD), F32).at[:, 0, :2 * REL_CLIP + 1].set(rel_bias)
    wide = ATT_HPS * ATT_HEAD_DIM
    (out,), side_outs = _call_with_sides(
        functools.partial(_attention_body, seq=s),
        grid=(ATT_GROUPS, b),
        in_specs=[
            pl.BlockSpec((None, s, wide), lambda h, i: (i, 0, h)),
            pl.BlockSpec((None, s, wide), lambda h, i: (i, 0, ATT_GROUPS + h)),
            pl.BlockSpec((None, s, wide), lambda h, i: (i, 0, 2 * ATT_GROUPS + h)),
            pl.BlockSpec((ATT_HPS, 8, REL_PAD), lambda h, i: (h, 0, 0)),
        ],
        args=[qkv, qkv, qkv, rb],
        out_specs=[pl.BlockSpec((None, s, wide), lambda h, i: (i, 0, h))],
        out_shapes=[jax.ShapeDtypeStruct((b, s, D_MODEL), BF16)],
        scratch_shapes=[pltpu.VMEM((ATT_HPS, ATT_LQ, ATT_WIN), F32)],
        sides=sides,
        params=_params(("arbitrary", "arbitrary"), 48),
        name="chunk_attention")
    return out, side_outs


def _ffn_body(te_ref, nu_ref, tr_ref, *refs, normalize, half_tiles):
    del te_ref, nu_ref
    if normalize:
        x_ref, g_ref, g_next_ref, wg_ref, wu_ref, wo_ref, o_ref, h_next_ref, xb_ref = refs
    else:
        x_ref, wg_ref, wu_ref, wo_ref, o_ref, xb_ref = refs
    tm = xb_ref.shape[0]

    @pl.when(pl.program_id(1) == 0)
    def _():
        x = x_ref[...]
        if normalize:
            ms = jnp.mean(x * x, axis=-1, keepdims=True)
            xb_ref[...] = (x * lax.rsqrt(ms + EPS) * g_ref[...]).astype(BF16)
            o_ref[...] = x
        else:
            xb_ref[...] = x.astype(BF16)
            o_ref[...] = jnp.zeros_like(o_ref)

    def accumulate(m):
        xb = xb_ref[:m, :]
        gate = jnp.dot(xb, wg_ref[...], preferred_element_type=F32)
        up = jnp.dot(xb, wu_ref[...], preferred_element_type=F32)
        act = (_silu(gate) * up).astype(BF16)
        o_ref[:m, :] += jnp.dot(act, wo_ref[...], preferred_element_type=F32)

    rows_in_tile = tr_ref[pl.program_id(0)]
    if half_tiles:
        @pl.when(rows_in_tile > tm // 2)
        def _():
            accumulate(tm)

        @pl.when(jnp.logical_and(rows_in_tile > 0, rows_in_tile <= tm // 2))
        def _():
            accumulate(tm // 2)
    else:
        @pl.when(rows_in_tile > 0)
        def _():
            accumulate(tm)

    if normalize:
        @pl.when(pl.program_id(1) == pl.num_programs(1) - 1)
        def _():
            o = o_ref[...]
            ms = jnp.mean(o * o, axis=-1, keepdims=True)
            h_next_ref[...] = (o * lax.rsqrt(ms + EPS) * g_next_ref[...]).astype(BF16)


def _ffn(x, gains, w_gate, w_up, w_out, tile_expert, n_used, tile_rows, tm, tf, vmem_mib, name,
         half_tiles=False, sides=()):
    rows, d = x.shape
    f = w_out.shape[1]
    nj = f // tf
    nt = rows // tm
    normalize = gains is not None

    def col(i, j, nu):
        return jnp.where(i < nu[0], j, nj - 1)

    def row_block():
        return pl.BlockSpec((tm, d), lambda i, j, te, nu, tr: (i, 0))

    in_specs = [pl.BlockSpec(
        (tm, d), lambda i, j, te, nu, tr: (jnp.maximum(jnp.minimum(i, nu[0] - 1), 0), 0))]
    args = [x]
    out_specs = [row_block()]
    out_shapes = [jax.ShapeDtypeStruct((rows, d), F32)]
    if normalize:
        for gain in gains:
            in_specs.append(pl.BlockSpec((1, d), lambda i, j, te, nu, tr: (0, 0)))
            args.append(gain.reshape(1, d))
        out_specs.append(row_block())
        out_shapes.append(jax.ShapeDtypeStruct((rows, d), BF16))
    in_specs += [
        pl.BlockSpec((None, d, tf), lambda i, j, te, nu, tr: (te[i], 0, col(i, j, nu))),
        pl.BlockSpec((None, d, tf), lambda i, j, te, nu, tr: (te[i], 0, col(i, j, nu))),
        pl.BlockSpec((None, tf, d), lambda i, j, te, nu, tr: (te[i], col(i, j, nu), 0)),
    ]
    args += [w_gate, w_up, w_out]
    return _call_with_sides(
        functools.partial(_ffn_body, normalize=normalize, half_tiles=half_tiles),
        grid=(nt, nj),
        in_specs=in_specs, args=args,
        out_specs=out_specs, out_shapes=out_shapes,
        scratch_shapes=[pltpu.VMEM((tm, d), BF16)],
        sides=sides,
        params=_params(("arbitrary", "arbitrary"), vmem_mib),
        name=name,
        prefetch=(tile_expert, n_used, tile_rows))


def _normed(x_ref, g_ref):
    x = x_ref[...]
    ms = jnp.mean(x * x, axis=-1, keepdims=True)
    return x * lax.rsqrt(ms + EPS) * g_ref[...]


def _router_body(x_ref, g_ref, wr_ref, o_ref, cnt_ref, carry_ref, *, tm):
    @pl.when(pl.program_id(0) == 0)
    def _():
        carry_ref[...] = jnp.zeros_like(carry_ref)

    def split(a):
        hi = a.astype(BF16)
        return hi, (a - hi.astype(F32)).astype(BF16)

    h_hi, h_lo = split(_normed(x_ref, g_ref))
    w_hi, w_lo = split(wr_ref[...])
    logits = (jnp.dot(h_hi, w_hi, preferred_element_type=F32)
              + jnp.dot(h_hi, w_lo, preferred_element_type=F32)
              + jnp.dot(h_lo, w_hi, preferred_element_type=F32))
    lt = logits.T[:N_EXPERTS, :]
    e = lax.broadcasted_iota(jnp.int32, (N_EXPERTS, tm), 0)
    m1 = jnp.max(lt, axis=0, keepdims=True)
    i1 = jnp.min(jnp.where(lt == m1, e, N_EXPERTS), axis=0, keepdims=True)
    lt2 = jnp.where(e == i1, -jnp.inf, lt)
    m2 = jnp.max(lt2, axis=0, keepdims=True)
    i2 = jnp.min(jnp.where(lt2 == m2, e, N_EXPERTS), axis=0, keepdims=True)
    ex = jnp.exp(m2 - m1)
    g1 = 1.0 / (1.0 + ex)
    g2 = ex / (1.0 + ex)

    oh1 = jnp.where(e == i1, 1.0, 0.0)
    oh2 = jnp.where(e == i2, 1.0, 0.0)
    before = jnp.where(lax.broadcasted_iota(jnp.int32, (tm, tm), 0)
                       < lax.broadcasted_iota(jnp.int32, (tm, tm), 1), 1.0, 0.0).astype(BF16)
    c1 = jnp.dot(oh1.astype(BF16), before, preferred_element_type=F32)
    c2 = jnp.dot(oh2.astype(BF16), before, preferred_element_type=F32)
    t1 = jnp.sum(oh1, axis=1, keepdims=True)
    t2 = jnp.sum(oh2, axis=1, keepdims=True)
    carry = carry_ref[...]
    base = carry[:, :1]
    r1 = jnp.sum(oh1 * (base + c1), axis=0, keepdims=True)
    r2 = jnp.sum(oh2 * (base + t1 + c2), axis=0, keepdims=True)
    carry = carry + t1 + t2
    carry_ref[...] = carry
    cnt_ref[...] = carry

    out = jnp.where(e == 0, i1.astype(F32),
          jnp.where(e == 1, i2.astype(F32),
          jnp.where(e == 2, g1,
          jnp.where(e == 3, g2,
          jnp.where(e == 4, r1,
          jnp.where(e == 5, r2, 0.0))))))
    o_ref[...] = out


def _router(x2, gain, w_router, tm=512):
    t, d = x2.shape
    wr = jnp.pad(w_router, ((0, 0), (0, LANES - N_EXPERTS)))
    return pl.pallas_call(
        functools.partial(_router_body, tm=tm),
        grid=(t // tm,),
        in_specs=[pl.BlockSpec((tm, d), lambda i: (i, 0)),
                  pl.BlockSpec((1, d), lambda i: (0, 0)),
                  pl.BlockSpec((d, LANES), lambda i: (0, 0))],
        out_specs=[pl.BlockSpec((N_EXPERTS, tm), lambda i: (0, i)),
                   pl.BlockSpec((N_EXPERTS, LANES), lambda i: (0, 0))],
        out_shape=[jax.ShapeDtypeStruct((N_EXPERTS, t), F32),
                   jax.ShapeDtypeStruct((N_EXPERTS, LANES), F32)],
        scratch_shapes=[pltpu.VMEM((N_EXPERTS, LANES), F32)],
        compiler_params=_params(("arbitrary",), 32),
        name="router_top2",
    )(x2, gain.reshape(1, d), wr)


def _dispatch_body(pos_ref, fill_ref, x_ref, g_ref, xs_ref, hbuf, zbuf, sem, zsem,
                   *, tk, tokens, n_tiles, n_steps):
    i = pl.program_id(0)
    slot = i % 2
    base = i * tk

    def tile_fill(tile):
        return pltpu.make_async_copy(zbuf, xs_ref.at[pl.ds(tile * MOE_TM, MOE_TM)], zsem)

    @pl.when(i == 0)
    def _():
        zbuf[...] = jnp.zeros_like(zbuf)

        def start_fill(tile, carry):
            @pl.when(fill_ref[tile] != 0)
            def _():
                tile_fill(tile).start()
            return carry

        lax.fori_loop(0, n_tiles, start_fill, 0)

        def wait_fill(tile, carry):
            @pl.when(fill_ref[tile] != 0)
            def _():
                tile_fill(tile).wait()
            return carry

        lax.fori_loop(0, n_tiles, wait_fill, 0)

    def wait_slot(s):
        for _ in range(2):
            pltpu.make_async_copy(hbuf.at[s], xs_ref.at[pl.ds(0, tk)], sem.at[s]).wait()

    @pl.when(i >= 2)
    def _():
        wait_slot(slot)

    hbuf[slot] = _normed(x_ref, g_ref)

    def start(group, carry):
        r0 = pl.multiple_of(group * 8, 8)
        for u in range(8):
            for assignment in range(2):
                dst = pos_ref[assignment * tokens + base + r0 + u]
                pltpu.make_async_copy(hbuf.at[slot, pl.ds(r0 + u, 1)], xs_ref.at[pl.ds(dst, 1)],
                                      sem.at[slot]).start()
        return carry

    lax.fori_loop(0, tk // 8, start, 0)

    @pl.when(i == n_steps - 1)
    def _():
        if n_steps >= 2:
            wait_slot(1 - slot)
        wait_slot(slot)


def _dispatch(x2, gain, pos_flat, fill_flag, n_tiles, tk=512):
    t, d = x2.shape
    return pl.pallas_call(
        functools.partial(_dispatch_body, tk=tk, tokens=t, n_tiles=n_tiles, n_steps=t // tk),
        grid_spec=pltpu.PrefetchScalarGridSpec(
            num_scalar_prefetch=2,
            grid=(t // tk,),
            in_specs=[pl.BlockSpec((tk, d), lambda i, pos, fill: (i, 0)),
                      pl.BlockSpec((1, d), lambda i, pos, fill: (0, 0))],
            out_specs=pl.BlockSpec(memory_space=pl.ANY),
            scratch_shapes=[pltpu.VMEM((2, tk, d), F32),
                            pltpu.VMEM((MOE_TM, d), F32),
                            pltpu.SemaphoreType.DMA((2,)),
                            pltpu.SemaphoreType.DMA(())]),
        out_shape=jax.ShapeDtypeStruct((n_tiles * MOE_TM, d), F32),
        compiler_params=_params(("arbitrary",), 40),
        name="moe_dispatch",
    )(pos_flat, fill_flag, x2, gain.reshape(1, d))


def _combine_body(pos_ref, x_ref, gates_ref, gf_ref, y_ref, o_ref, ybuf, sem,
                  *, tk, tokens, n_steps):
    i = pl.program_id(0)
    slot = i % 2

    def gather(step, s):
        base = step * tk

        def start(r, carry):
            for assignment in range(2):
                src = pos_ref[assignment * tokens + base + r]
                pltpu.make_async_copy(y_ref.at[pl.ds(src, 1)],
                                      ybuf.at[s, assignment, pl.ds(r, 1)], sem.at[s]).start()
            return carry

        lax.fori_loop(0, tk, start, 0, unroll=8)

    @pl.when(i == 0)
    def _():
        gather(0, 0)

    @pl.when(i + 1 < n_steps)
    def _():
        gather(i + 1, 1 - slot)

    for assignment in range(2):
        pltpu.make_async_copy(y_ref.at[pl.ds(0, tk)], ybuf.at[slot, assignment],
                              sem.at[slot]).wait()

    gates = gates_ref[...]
    x = x_ref[...] + gates[:, 0:1] * ybuf[slot, 0] + gates[:, 1:2] * ybuf[slot, 1]
    ms = jnp.mean(x * x, axis=-1, keepdims=True)
    o_ref[...] = x * lax.rsqrt(ms + EPS) * gf_ref[...]


def _combine(x2, y, pos_flat, gates, g_final, tk=256):
    t, d = x2.shape
    return pl.pallas_call(
        functools.partial(_combine_body, tk=tk, tokens=t, n_steps=t // tk),
        grid_spec=pltpu.PrefetchScalarGridSpec(
            num_scalar_prefetch=1,
            grid=(t // tk,),
            in_specs=[pl.BlockSpec((tk, d), lambda i, pos: (i, 0)),
                      pl.BlockSpec((tk, 2), lambda i, pos: (i, 0)),
                      pl.BlockSpec((1, d), lambda i, pos: (0, 0)),
                      pl.BlockSpec(memory_space=pl.ANY)],
            out_specs=pl.BlockSpec((tk, d), lambda i, pos: (i, 0)),
            scratch_shapes=[pltpu.VMEM((2, 2, tk, d), F32),
                            pltpu.SemaphoreType.DMA((2,))]),
        out_shape=jax.ShapeDtypeStruct((t, d), F32),
        compiler_params=_params(("arbitrary",), 48),
        name="moe_combine",
    )(pos_flat, x2, gates, g_final.reshape(1, d), y)


MOE_TM = 512
DENSE_TF = 512
EXPERT_TF = 1024
PROJ_TM = 1024
PROJ_TN = 1024
MOE_GATE_CAST_BLOCKS = 128
MOE_OUT_CAST_BLOCKS = 256


def _routing_tables(route, counts, n_tiles):
    idx1 = route[0].astype(jnp.int32)
    idx2 = route[1].astype(jnp.int32)
    cnt = counts[:, 0].astype(jnp.int32)
    tiles_e = (cnt + MOE_TM - 1) // MOE_TM
    tile_end = jnp.cumsum(tiles_e)
    offset = (tile_end - tiles_e) * MOE_TM
    pos1 = offset[idx1] + route[4].astype(jnp.int32)
    pos2 = offset[idx2] + route[5].astype(jnp.int32)
    n_used = tile_end[-1]
    tiles = jnp.arange(n_tiles, dtype=jnp.int32)
    te = jnp.sum((tiles[:, None] >= tile_end[None, :]).astype(jnp.int32), axis=1)
    te = jnp.minimum(te, N_EXPERTS - 1)
    te = jnp.where(tiles < n_used, te, te[n_used - 1])
    last_of_expert = jnp.any((tiles[:, None] == tile_end[None, :] - 1) & (tiles_e[None, :] > 0),
                             axis=1)
    fill_flag = (last_of_expert | (tiles >= n_used)).astype(jnp.int32)
    live = cnt[te] - (tiles - (tile_end - tiles_e)[te]) * MOE_TM
    tile_rows = jnp.where(tiles < n_used, jnp.clip(live, 0, MOE_TM), 0).astype(jnp.int32)
    pos_flat = jnp.concatenate([pos1, pos2]).astype(jnp.int32)
    gates = jnp.stack([route[2], route[3]], axis=1)
    return pos_flat, gates, te, n_used.reshape(1).astype(jnp.int32), fill_flag, tile_rows


def kernel(x, ln_mix, ln_ffn, ln_final, ret_w_in, ret_w_out, ffn_w_in, ffn_w_out,
           att_w_in, att_w_out, att_rel_bias, moe_router, moe_w_in, moe_w_out):
    b, s, d = x.shape
    t = b * s
    x2 = x.reshape(t, d)
    f_dense = ffn_w_out.shape[1]
    f_expert = moe_w_out.shape[2]

    moe_in2 = moe_w_in[0].reshape(N_EXPERTS * d, 2 * f_expert)
    moe_out2 = moe_w_out[0].reshape(N_EXPERTS * f_expert, d)
    ret_steps = b * RET_HEADS
    att_steps = ATT_GROUPS * b
    cast_moe_gate = _SideCast(moe_in2, MOE_GATE_CAST_BLOCKS, f_expert, 0)
    cast_moe_up = _SideCast(moe_in2, att_steps, f_expert, 1)
    cast_moe_out = _SideCast(moe_out2, MOE_OUT_CAST_BLOCKS, d)
    cast_ffn_gate = _SideCast(ffn_w_in[0], ret_steps, f_dense, 0)
    cast_ffn_up = _SideCast(ffn_w_in[0], ret_steps, f_dense, 1)
    cast_ffn_out = _SideCast(ffn_w_out[0], ret_steps // 2, d)

    h = _rmsnorm(x2, ln_mix[0], BF16)
    proj, (moe_gate_bf,) = _matmul(h, ret_w_in[0], None, BF16, PROJ_TM, PROJ_TN, "ret_proj_in",
                                   sides=[cast_moe_gate])
    pos = jnp.arange(s, dtype=F32)
    inv_freq = ROPE_BASE ** (-jnp.arange(0, RET_QK_DIM, 2, dtype=F32) / RET_QK_DIM)
    ang = pos[:, None] * inv_freq[None, :]
    y, (ffn_gate_bf, ffn_up_bf, ffn_out_bf) = _retention(
        proj.reshape(b, s, -1), jnp.cos(ang), jnp.sin(ang),
        sides=[cast_ffn_gate, cast_ffn_up, cast_ffn_out])
    x2, _ = _matmul(y.reshape(t, RET_HV), ret_w_out[0], x2, F32, PROJ_TM, PROJ_TN // 2,
                    "ret_proj_out")

    dense_tiles = t // MOE_TM
    (x2, h), (moe_out_bf,) = _ffn(
        x2, (ln_ffn[0], ln_mix[1]), ffn_gate_bf[None], ffn_up_bf[None], ffn_out_bf[None],
        jnp.zeros((dense_tiles,), jnp.int32), jnp.full((1,), dense_tiles, jnp.int32),
        jnp.full((dense_tiles,), MOE_TM, jnp.int32),
        MOE_TM, DENSE_TF, 48, "dense_swiglu", sides=[cast_moe_out])

    qkv, _ = _matmul(h, att_w_in[0], None, BF16, PROJ_TM, PROJ_TN, "att_proj_in")
    a, (moe_up_bf,) = _attention(qkv.reshape(b, s, -1), att_rel_bias[0], sides=[cast_moe_up])
    x2, _ = _matmul(a.reshape(t, d), att_w_out[0], x2, F32, PROJ_TM, PROJ_TN, "att_proj_out")

    route, counts = _router(x2, ln_ffn[1], moe_router[0])
    n_tiles = 2 * t // MOE_TM + N_EXPERTS
    pos_flat, gates, te, n_used, fill_flag, tile_rows = _routing_tables(route, counts, n_tiles)
    xs = _dispatch(x2, ln_ffn[1], pos_flat, fill_flag, n_tiles)
    (ys,), _ = _ffn(xs, None, moe_gate_bf.reshape(N_EXPERTS, d, f_expert),
                 moe_up_bf.reshape(N_EXPERTS, d, f_expert),
                 moe_out_bf.reshape(N_EXPERTS, f_expert, d), te, n_used, tile_rows,
                 MOE_TM, EXPERT_TF, 56, "expert_swiglu", half_tiles=True)
    out = _combine(x2, ys, pos_flat, gates, ln_final)
    return out.reshape(b, s, d)
```

```python
import functools
import math
from typing import NamedTuple

import jax
import jax.numpy as jnp
from jax import lax
from jax.experimental import pallas as pl
from jax.experimental.pallas import tpu as pltpu

D_MODEL = 2048
CHUNK = 64
EPS = 1e-6
RET_HEADS = 8
RET_QK_DIM = D_MODEL // RET_HEADS
RET_V_DIM = 2 * RET_QK_DIM
RET_HV = RET_HEADS * RET_V_DIM
ROPE_BASE = 10000.0
ATT_HEADS = 16
ATT_HEAD_DIM = D_MODEL // ATT_HEADS
LEFT_CHUNKS = 8
REL_CLIP = 128
NEG_INF = -1e30
N_EXPERTS = 8

LANES = 128
MIB = 1024 * 1024

F32 = jnp.float32
BF16 = jnp.bfloat16


def _params(semantics, vmem_mib):
    return pltpu.CompilerParams(
        dimension_semantics=semantics, vmem_limit_bytes=vmem_mib * MIB)


def _silu(x):
    return x * (1.0 / (1.0 + jnp.exp(-x)))


class _SideCast(NamedTuple):
    src: jax.Array
    num_blocks: int
    block_cols: int
    col_block: int = 0


def _call_with_sides(body, *, grid, in_specs, args, out_specs, out_shapes, scratch_shapes,
                     sides, params, name, prefetch=()):
    steps = math.prod(grid)
    n_pre, n_in, n_out, n_side = len(prefetch), len(args), len(out_shapes), len(sides)
    in_specs, args = list(in_specs), list(args)
    out_specs, out_shapes = list(out_specs), list(out_shapes)

    def block_spec(side, col_block):
        def index(*ids):
            step = ids[0]
            for extent, pid in zip(grid[1:], ids[1:len(grid)]):
                step = step * extent + pid
            return ((step * side.num_blocks) // steps, col_block)
        return pl.BlockSpec((side.src.shape[0] // side.num_blocks, side.block_cols), index)

    for side in sides:
        assert side.num_blocks <= steps and side.src.shape[0] % side.num_blocks == 0
        in_specs.append(block_spec(side, side.col_block))
        args.append(side.src)
        out_specs.append(block_spec(side, 0))
        out_shapes.append(jax.ShapeDtypeStruct((side.src.shape[0], side.block_cols), BF16))

    def wrapped(*refs):
        pre, refs = refs[:n_pre], refs[n_pre:]
        ins = refs[:n_in]
        srcs = refs[n_in:n_in + n_side]
        outs = refs[n_in + n_side:n_in + n_side + n_out]
        side_outs = refs[n_in + n_side + n_out:n_in + 2 * n_side + n_out]
        scratch = refs[n_in + 2 * n_side + n_out:]
        for src_ref, dst_ref in zip(srcs, side_outs):
            dst_ref[...] = src_ref[...].astype(BF16)
        body(*pre, *ins, *outs, *scratch)

    res = pl.pallas_call(
        wrapped,
        grid_spec=pltpu.PrefetchScalarGridSpec(
            num_scalar_prefetch=n_pre, grid=grid, in_specs=in_specs, out_specs=out_specs,
            scratch_shapes=scratch_shapes),
        out_shape=out_shapes, compiler_params=params, name=name)(*prefetch, *args)
    return res[:n_out], res[n_out:]


def _rmsnorm_body(x_ref, g_ref, o_ref):
    x = x_ref[...]
    ms = jnp.mean(x * x, axis=-1, keepdims=True)
    o_ref[...] = (x * lax.rsqrt(ms + EPS) * g_ref[...]).astype(o_ref.dtype)


def _rmsnorm(x2, g, out_dtype, tm=512):
    t, d = x2.shape
    return pl.pallas_call(
        _rmsnorm_body,
        grid=(t // tm,),
        in_specs=[pl.BlockSpec((tm, d), lambda i: (i, 0)),
                  pl.BlockSpec((1, d), lambda i: (0, 0))],
        out_specs=pl.BlockSpec((tm, d), lambda i: (i, 0)),
        out_shape=jax.ShapeDtypeStruct((t, d), out_dtype),
        compiler_params=_params(("arbitrary",), 32),
        name="rmsnorm",
    )(x2, g.reshape(1, d))


def _matmul_body(*refs, has_res):
    if has_res:
        x_ref, w_ref, r_ref, o_ref, wb_ref = refs
    else:
        x_ref, w_ref, o_ref, wb_ref = refs

    @pl.when(pl.program_id(1) == 0)
    def _():
        wb_ref[...] = w_ref[...].astype(BF16)

    acc = jnp.dot(x_ref[...], wb_ref[...], preferred_element_type=F32)
    if has_res:
        acc = acc + r_ref[...]
    o_ref[...] = acc.astype(o_ref.dtype)


def _matmul(x, w, res, out_dtype, tm, tn, name, sides=()):
    m, k = x.shape
    n = w.shape[1]
    in_specs = [pl.BlockSpec((tm, k), lambda j, i: (i, 0)),
                pl.BlockSpec((k, tn), lambda j, i: (0, j))]
    args = [x, w]
    if res is not None:
        in_specs.append(pl.BlockSpec((tm, tn), lambda j, i: (i, j)))
        args.append(res)
    (out,), side_outs = _call_with_sides(
        functools.partial(_matmul_body, has_res=res is not None),
        grid=(n // tn, m // tm),
        in_specs=in_specs, args=args,
        out_specs=[pl.BlockSpec((tm, tn), lambda j, i: (i, j))],
        out_shapes=[jax.ShapeDtypeStruct((m, n), out_dtype)],
        scratch_shapes=[pltpu.VMEM((k, tn), BF16)],
        sides=sides,
        params=_params(("arbitrary", "arbitrary"), 52),
        name=name)
    return out, side_outs


def _retention_body(q_ref, k_ref, v_ref, g_ref, cos_ref, sin_ref, o_ref, state_ref,
                    *, block_rows, seq):
    L = block_rows
    half = RET_QK_DIM // 2
    head = pl.program_id(1)

    def log_gamma(rows):
        headf = (jnp.zeros((rows, 1), jnp.int32) + head).astype(F32)
        return jnp.log(1.0 - jnp.exp2(-5.0 - headf))

    log_g = log_gamma(L)
    n_i = lax.broadcasted_iota(jnp.int32, (L, L), 0)
    m_i = lax.broadcasted_iota(jnp.int32, (L, L), 1)
    dist = jnp.abs(n_i - m_i).astype(F32)
    visible = (m_i >> 6) <= (n_i >> 6)
    decay = jnp.where(visible, jnp.exp(log_g * dist), 0.0)
    r = lax.broadcasted_iota(jnp.int32, (L, 1), 0).astype(F32)
    q_decay = jnp.exp(log_g * (r + 1.0))
    k_decay = jnp.exp(log_g * (L - 1.0 - r))
    block_decay = jnp.exp(log_gamma(RET_QK_DIM) * float(L))
    scale = RET_QK_DIM ** -0.5

    state_ref[...] = jnp.zeros_like(state_ref)

    def body(s, carry):
        r0 = pl.multiple_of(s * L, L)
        cos = cos_ref[pl.ds(r0, L), :]
        sin = sin_ref[pl.ds(r0, L), :]

        def rot(t):
            t1 = t[:, :half]
            t2 = t[:, half:]
            return jnp.concatenate([t1 * cos - t2 * sin, t1 * sin + t2 * cos], axis=1)

        q = rot(q_ref[pl.ds(r0, L), :].astype(F32)).astype(BF16)
        kf = rot(k_ref[pl.ds(r0, L), :].astype(F32)) * scale
        v = v_ref[pl.ds(r0, L), :]
        s_qk = lax.dot_general(q, kf.astype(BF16), (((1,), (1,)), ((), ())),
                               preferred_element_type=F32) * decay
        inner = jnp.dot(s_qk.astype(BF16), v, preferred_element_type=F32)
        st = state_ref[...]
        cross = jnp.dot(q, st.astype(BF16), preferred_element_type=F32) * q_decay
        kd_t = (kf * k_decay).T.astype(BF16)
        state_ref[...] = st * block_decay + jnp.dot(kd_t, v, preferred_element_type=F32)
        o = inner + cross
        o = o * lax.rsqrt(jnp.mean(o * o, axis=-1, keepdims=True) + EPS)
        gate = g_ref[pl.ds(r0, L), :].astype(F32)
        o_ref[pl.ds(r0, L), :] = (_silu(gate) * o).astype(o_ref.dtype)
        return carry

    lax.fori_loop(0, seq // L, body, 0, unroll=2)


def _retention(proj, cos, sin, sides=(), block_rows=256):
    b, s, _ = proj.shape
    nq = D_MODEL // RET_QK_DIM
    nv = RET_HV // RET_V_DIM
    (out,), side_outs = _call_with_sides(
        functools.partial(_retention_body, block_rows=block_rows, seq=s),
        grid=(b, RET_HEADS),
        in_specs=[
            pl.BlockSpec((None, s, RET_QK_DIM), lambda i, h: (i, 0, h)),
            pl.BlockSpec((None, s, RET_QK_DIM), lambda i, h: (i, 0, nq + h)),
            pl.BlockSpec((None, s, RET_V_DIM), lambda i, h: (i, 0, nv + h)),
            pl.BlockSpec((None, s, RET_V_DIM), lambda i, h: (i, 0, 2 * nv + h)),
            pl.BlockSpec((s, RET_QK_DIM // 2), lambda i, h: (0, 0)),
            pl.BlockSpec((s, RET_QK_DIM // 2), lambda i, h: (0, 0)),
        ],
        args=[proj, proj, proj, proj, cos, sin],
        out_specs=[pl.BlockSpec((None, s, RET_V_DIM), lambda i, h: (i, 0, h))],
        out_shapes=[jax.ShapeDtypeStruct((b, s, RET_HV), BF16)],
        scratch_shapes=[pltpu.VMEM((RET_QK_DIM, RET_V_DIM), F32)],
        sides=sides,
        params=_params(("arbitrary", "arbitrary"), 48),
        name="retention_core")
    return out, side_outs


ATT_PAD = LEFT_CHUNKS * CHUNK
ATT_LQ = 256
ATT_WIN = ATT_LQ + ATT_PAD
ATT_ROLL_W = ATT_LQ + ATT_WIN
REL_PAD = 384


ATT_HPS = 2
ATT_GROUPS = ATT_HEADS // ATT_HPS


def _attention_body(q_ref, k_ref, v_ref, rb_ref, o_ref, bm_ref, *, seq):
    @pl.when(pl.program_id(1) == 0)
    def _():
        t = lax.broadcasted_iota(jnp.int32, (REL_PAD, ATT_ROLL_W), 1)
        j = lax.broadcasted_iota(jnp.int32, (REL_PAD, ATT_ROLL_W), 0)
        idx = jnp.where(t <= ATT_WIN, jnp.clip(ATT_PAD - t, -REL_CLIP, REL_CLIP),
                        REL_CLIP) + REL_CLIP
        onehot = jnp.where(idx == j, 1.0, 0.0).astype(BF16)
        row = lax.broadcasted_iota(jnp.int32, (ATT_LQ, ATT_ROLL_W), 0)
        qc = lax.broadcasted_iota(jnp.int32, (ATT_LQ, ATT_WIN), 0) >> 6
        kc = lax.broadcasted_iota(jnp.int32, (ATT_LQ, ATT_WIN), 1) >> 6
        ahead = kc - qc
        for hh in range(ATT_HPS):
            rb = rb_ref[hh]
            hi = rb.astype(BF16)
            rest = rb - hi.astype(F32)
            mid = rest.astype(BF16)
            lo = (rest - mid.astype(F32)).astype(BF16)
            g = (jnp.dot(hi, onehot, preferred_element_type=F32)
                 + jnp.dot(mid, onehot, preferred_element_type=F32)
                 + jnp.dot(lo, onehot, preferred_element_type=F32))
            rows = jnp.broadcast_to(g[0:1, :], (ATT_LQ, ATT_ROLL_W))
            shift = 1
            while shift < ATT_LQ:
                rows = jnp.where((row & shift) != 0, pltpu.roll(rows, shift, 1), rows)
                shift *= 2
            banded = jnp.where(ahead <= LEFT_CHUNKS, rows[:, :ATT_WIN], NEG_INF)
            bm_ref[hh] = jnp.where(ahead >= 0, banded, NEG_INF)

    scale = ATT_HEAD_DIM ** -0.5
    for q0 in range(0, seq, ATT_LQ):
        k_lo = max(0, q0 - ATT_PAD)
        width = q0 + ATT_LQ - k_lo
        for hh in range(ATT_HPS):
            cols = slice(hh * ATT_HEAD_DIM, (hh + 1) * ATT_HEAD_DIM)
            q = q_ref[q0:q0 + ATT_LQ, cols]
            kw = k_ref[k_lo:k_lo + width, cols]
            vw = v_ref[k_lo:k_lo + width, cols]
            logits = lax.dot_general(q, kw, (((1,), (1,)), ((), ())),
                                     preferred_element_type=F32) * scale
            logits = logits + bm_ref[hh, :, ATT_WIN - width:]
            mx = jnp.max(logits, axis=-1, keepdims=True)
            p = jnp.exp(logits - mx)
            denom = jnp.sum(p, axis=-1, keepdims=True)
            o = jnp.dot(p.astype(BF16), vw, preferred_element_type=F32) / denom
            o_ref[q0:q0 + ATT_LQ, cols] = o.astype(o_ref.dtype)


def _attention(qkv, rel_bias, sides=()):
    b, s, _ = qkv.shape
    rb = jnp.zeros((ATT_HEADS, 8, REL_PAD), F32).at[:, 0, :2 * REL_CLIP + 1].set(rel_bias)
    wide = ATT_HPS * ATT_HEAD_DIM
    (out,), side_outs = _call_with_sides(
        functools.partial(_attention_body, seq=s),
        grid=(ATT_GROUPS, b),
        in_specs=[
            pl.BlockSpec((None, s, wide), lambda h, i: (i, 0, h)),
            pl.BlockSpec((None, s, wide), lambda h, i: (i, 0, ATT_GROUPS + h)),
            pl.BlockSpec((None, s, wide), lambda h, i: (i, 0, 2 * ATT_GROUPS + h)),
            pl.BlockSpec((ATT_HPS, 8, REL_PAD), lambda h, i: (h, 0, 0)),
        ],
        args=[qkv, qkv, qkv, rb],
        out_specs=[pl.BlockSpec((None, s, wide), lambda h, i: (i, 0, h))],
        out_shapes=[jax.ShapeDtypeStruct((b, s, D_MODEL), BF16)],
        scratch_shapes=[pltpu.VMEM((ATT_HPS, ATT_LQ, ATT_WIN), F32)],
        sides=sides,
        params=_params(("arbitrary", "arbitrary"), 48),
        name="chunk_attention")
    return out, side_outs


def _ffn_body(te_ref, nu_ref, tr_ref, *refs, normalize, half_tiles):
    del te_ref, nu_ref
    if normalize:
        x_ref, g_ref, g_next_ref, wg_ref, wu_ref, wo_ref, o_ref, h_next_ref, xb_ref = refs
    else:
        x_ref, wg_ref, wu_ref, wo_ref, o_ref, xb_ref = refs
    tm = xb_ref.shape[0]

    @pl.when(pl.program_id(1) == 0)
    def _():
        x = x_ref[...]
        if normalize:
            ms = jnp.mean(x * x, axis=-1, keepdims=True)
            xb_ref[...] = (x * lax.rsqrt(ms + EPS) * g_ref[...]).astype(BF16)
            o_ref[...] = x
        else:
            xb_ref[...] = x.astype(BF16)
            o_ref[...] = jnp.zeros_like(o_ref)

    def accumulate(m):
        xb = xb_ref[:m, :]
        gate = jnp.dot(xb, wg_ref[...], preferred_element_type=F32)
        up = jnp.dot(xb, wu_ref[...], preferred_element_type=F32)
        act = (_silu(gate) * up).astype(BF16)
        o_ref[:m, :] += jnp.dot(act, wo_ref[...], preferred_element_type=F32)

    rows_in_tile = tr_ref[pl.program_id(0)]
    if half_tiles:
        @pl.when(rows_in_tile > tm // 2)
        def _():
            accumulate(tm)

        @pl.when(jnp.logical_and(rows_in_tile > 0, rows_in_tile <= tm // 2))
        def _():
            accumulate(tm // 2)
    else:
        @pl.when(rows_in_tile > 0)
        def _():
            accumulate(tm)

    if normalize:
        @pl.when(pl.program_id(1) == pl.num_programs(1) - 1)
        def _():
            o = o_ref[...]
            ms = jnp.mean(o * o, axis=-1, keepdims=True)
            h_next_ref[...] = (o * lax.rsqrt(ms + EPS) * g_next_ref[...]).astype(BF16)


def _ffn(x, gains, w_gate, w_up, w_out, tile_expert, n_used, tile_rows, tm, tf, vmem_mib, name,
         half_tiles=False, sides=()):
    rows, d = x.shape
    f = w_out.shape[1]
    nj = f // tf
    nt = rows // tm
    normalize = gains is not None

    def col(i, j, nu):
        return jnp.where(i < nu[0], j, nj - 1)

    def row_block():
        return pl.BlockSpec((tm, d), lambda i, j, te, nu, tr: (i, 0))

    in_specs = [pl.BlockSpec(
        (tm, d), lambda i, j, te, nu, tr: (jnp.maximum(jnp.minimum(i, nu[0] - 1), 0), 0))]
    args = [x]
    out_specs = [row_block()]
    out_shapes = [jax.ShapeDtypeStruct((rows, d), F32)]
    if normalize:
        for gain in gains:
            in_specs.append(pl.BlockSpec((1, d), lambda i, j, te, nu, tr: (0, 0)))
            args.append(gain.reshape(1, d))
        out_specs.append(row_block())
        out_shapes.append(jax.ShapeDtypeStruct((rows, d), BF16))
    in_specs += [
        pl.BlockSpec((None, d, tf), lambda i, j, te, nu, tr: (te[i], 0, col(i, j, nu))),
        pl.BlockSpec((None, d, tf), lambda i, j, te, nu, tr: (te[i], 0, col(i, j, nu))),
        pl.BlockSpec((None, tf, d), lambda i, j, te, nu, tr: (te[i], col(i, j, nu), 0)),
    ]
    args += [w_gate, w_up, w_out]
    return _call_with_sides(
        functools.partial(_ffn_body, normalize=normalize, half_tiles=half_tiles),
        grid=(nt, nj),
        in_specs=in_specs, args=args,
        out_specs=out_specs, out_shapes=out_shapes,
        scratch_shapes=[pltpu.VMEM((tm, d), BF16)],
        sides=sides,
        params=_params(("arbitrary", "arbitrary"), vmem_mib),
        name=name,
        prefetch=(tile_expert, n_used, tile_rows))


def _normed(x_ref, g_ref):
    x = x_ref[...]
    ms = jnp.mean(x * x, axis=-1, keepdims=True)
    return x * lax.rsqrt(ms + EPS) * g_ref[...]


def _router_body(x_ref, g_ref, wr_ref, o_ref, cnt_ref, carry_ref, *, tm):
    @pl.when(pl.program_id(0) == 0)
    def _():
        carry_ref[...] = jnp.zeros_like(carry_ref)

    def split(a):
        hi = a.astype(BF16)
        return hi, (a - hi.astype(F32)).astype(BF16)

    h_hi, h_lo = split(_normed(x_ref, g_ref))
    w_hi, w_lo = split(wr_ref[...])
    logits = (jnp.dot(h_hi, w_hi, preferred_element_type=F32)
              + jnp.dot(h_hi, w_lo, preferred_element_type=F32)
              + jnp.dot(h_lo, w_hi, preferred_element_type=F32))
    lt = logits.T[:N_EXPERTS, :]
    e = lax.broadcasted_iota(jnp.int32, (N_EXPERTS, tm), 0)
    m1 = jnp.max(lt, axis=0, keepdims=True)
    i1 = jnp.min(jnp.where(lt == m1, e, N_EXPERTS), axis=0, keepdims=True)
    lt2 = jnp.where(e == i1, -jnp.inf, lt)
    m2 = jnp.max(lt2, axis=0, keepdims=True)
    i2 = jnp.min(jnp.where(lt2 == m2, e, N_EXPERTS), axis=0, keepdims=True)
    ex = jnp.exp(m2 - m1)
    g1 = 1.0 / (1.0 + ex)
    g2 = ex / (1.0 + ex)

    oh1 = jnp.where(e == i1, 1.0, 0.0)
    oh2 = jnp.where(e == i2, 1.0, 0.0)
    before = jnp.where(lax.broadcasted_iota(jnp.int32, (tm, tm), 0)
                       < lax.broadcasted_iota(jnp.int32, (tm, tm), 1), 1.0, 0.0).astype(BF16)
    c1 = jnp.dot(oh1.astype(BF16), before, preferred_element_type=F32)
    c2 = jnp.dot(oh2.astype(BF16), before, preferred_element_type=F32)
    t1 = jnp.sum(oh1, axis=1, keepdims=True)
    t2 = jnp.sum(oh2, axis=1, keepdims=True)
    carry = carry_ref[...]
    base = carry[:, :1]
    r1 = jnp.sum(oh1 * (base + c1), axis=0, keepdims=True)
    r2 = jnp.sum(oh2 * (base + t1 + c2), axis=0, keepdims=True)
    carry = carry + t1 + t2
    carry_ref[...] = carry
    cnt_ref[...] = carry

    out = jnp.where(e == 0, i1.astype(F32),
          jnp.where(e == 1, i2.astype(F32),
          jnp.where(e == 2, g1,
          jnp.where(e == 3, g2,
          jnp.where(e == 4, r1,
          jnp.where(e == 5, r2, 0.0))))))
    o_ref[...] = out


def _router(x2, gain, w_router, tm=512):
    t, d = x2.shape
    wr = jnp.pad(w_router, ((0, 0), (0, LANES - N_EXPERTS)))
    return pl.pallas_call(
        functools.partial(_router_body, tm=tm),
        grid=(t // tm,),
        in_specs=[pl.BlockSpec((tm, d), lambda i: (i, 0)),
                  pl.BlockSpec((1, d), lambda i: (0, 0)),
                  pl.BlockSpec((d, LANES), lambda i: (0, 0))],
        out_specs=[pl.BlockSpec((N_EXPERTS, tm), lambda i: (0, i)),
                   pl.BlockSpec((N_EXPERTS, LANES), lambda i: (0, 0))],
        out_shape=[jax.ShapeDtypeStruct((N_EXPERTS, t), F32),
                   jax.ShapeDtypeStruct((N_EXPERTS, LANES), F32)],
        scratch_shapes=[pltpu.VMEM((N_EXPERTS, LANES), F32)],
        compiler_params=_params(("arbitrary",), 32),
        name="router_top2",
    )(x2, gain.reshape(1, d), wr)


def _dispatch_body(pos_ref, fill_ref, x_ref, g_ref, xs_ref, hbuf, zbuf, sem, zsem,
                   *, tk, tokens, n_tiles, n_steps):
    i = pl.program_id(0)
    slot = i % 2
    base = i * tk

    def tile_fill(tile):
        return pltpu.make_async_copy(zbuf, xs_ref.at[pl.ds(tile * MOE_TM, MOE_TM)], zsem)

    @pl.when(i == 0)
    def _():
        zbuf[...] = jnp.zeros_like(zbuf)

        def start_fill(tile, carry):
            @pl.when(fill_ref[tile] != 0)
            def _():
                tile_fill(tile).start()
            return carry

        lax.fori_loop(0, n_tiles, start_fill, 0)

        def wait_fill(tile, carry):
            @pl.when(fill_ref[tile] != 0)
            def _():
                tile_fill(tile).wait()
            return carry

        lax.fori_loop(0, n_tiles, wait_fill, 0)

    def wait_slot(s):
        for _ in range(2):
            pltpu.make_async_copy(hbuf.at[s], xs_ref.at[pl.ds(0, tk)], sem.at[s]).wait()

    @pl.when(i >= 2)
    def _():
        wait_slot(slot)

    hbuf[slot] = _normed(x_ref, g_ref)

    def start(group, carry):
        r0 = pl.multiple_of(group * 8, 8)
        for u in range(8):
            for assignment in range(2):
                dst = pos_ref[assignment * tokens + base + r0 + u]
                pltpu.make_async_copy(hbuf.at[slot, pl.ds(r0 + u, 1)], xs_ref.at[pl.ds(dst, 1)],
                                      sem.at[slot]).start(priority=assignment)
        return carry

    lax.fori_loop(0, tk // 8, start, 0)

    @pl.when(i == n_steps - 1)
    def _():
        if n_steps >= 2:
            wait_slot(1 - slot)
        wait_slot(slot)


def _dispatch(x2, gain, pos_flat, fill_flag, n_tiles, tk=512):
    t, d = x2.shape
    return pl.pallas_call(
        functools.partial(_dispatch_body, tk=tk, tokens=t, n_tiles=n_tiles, n_steps=t // tk),
        grid_spec=pltpu.PrefetchScalarGridSpec(
            num_scalar_prefetch=2,
            grid=(t // tk,),
            in_specs=[pl.BlockSpec((tk, d), lambda i, pos, fill: (i, 0)),
                      pl.BlockSpec((1, d), lambda i, pos, fill: (0, 0))],
            out_specs=pl.BlockSpec(memory_space=pl.ANY),
            scratch_shapes=[pltpu.VMEM((2, tk, d), F32),
                            pltpu.VMEM((MOE_TM, d), F32),
                            pltpu.SemaphoreType.DMA((2,)),
                            pltpu.SemaphoreType.DMA(())]),
        out_shape=jax.ShapeDtypeStruct((n_tiles * MOE_TM, d), F32),
        compiler_params=_params(("arbitrary",), 40),
        name="moe_dispatch",
    )(pos_flat, fill_flag, x2, gain.reshape(1, d))


def _combine_body(pos_ref, x_ref, gates_ref, gf_ref, y_ref, o_ref, ybuf, sem,
                  *, tk, tokens, n_steps):
    i = pl.program_id(0)
    slot = i % 2

    def gather(step, s):
        base = step * tk

        def start(r, carry):
            for assignment in range(2):
                src = pos_ref[assignment * tokens + base + r]
                pltpu.make_async_copy(y_ref.at[pl.ds(src, 1)],
                                      ybuf.at[s, assignment, pl.ds(r, 1)],
                                      sem.at[s]).start(priority=assignment)
            return carry

        lax.fori_loop(0, tk, start, 0, unroll=8)

    @pl.when(i == 0)
    def _():
        gather(0, 0)

    @pl.when(i + 1 < n_steps)
    def _():
        gather(i + 1, 1 - slot)

    for assignment in range(2):
        pltpu.make_async_copy(y_ref.at[pl.ds(0, tk)], ybuf.at[slot, assignment],
                              sem.at[slot]).wait()

    gates = gates_ref[...]
    x = x_ref[...] + gates[:, 0:1] * ybuf[slot, 0] + gates[:, 1:2] * ybuf[slot, 1]
    ms = jnp.mean(x * x, axis=-1, keepdims=True)
    o_ref[...] = x * lax.rsqrt(ms + EPS) * gf_ref[...]


def _combine(x2, y, pos_flat, gates, g_final, tk=256):
    t, d = x2.shape
    return pl.pallas_call(
        functools.partial(_combine_body, tk=tk, tokens=t, n_steps=t // tk),
        grid_spec=pltpu.PrefetchScalarGridSpec(
            num_scalar_prefetch=1,
            grid=(t // tk,),
            in_specs=[pl.BlockSpec((tk, d), lambda i, pos: (i, 0)),
                      pl.BlockSpec((tk, 2), lambda i, pos: (i, 0)),
                      pl.BlockSpec((1, d), lambda i, pos: (0, 0)),
                      pl.BlockSpec(memory_space=pl.ANY)],
            out_specs=pl.BlockSpec((tk, d), lambda i, pos: (i, 0)),
            scratch_shapes=[pltpu.VMEM((2, 2, tk, d), F32),
                            pltpu.SemaphoreType.DMA((2,))]),
        out_shape=jax.ShapeDtypeStruct((t, d), F32),
        compiler_params=_params(("arbitrary",), 48),
        name="moe_combine",
    )(pos_flat, x2, gates, g_final.reshape(1, d), y)


MOE_TM = 512
DENSE_TF = 512
EXPERT_TF = 1024
PROJ_TM = 1024
PROJ_TN = 1024
MOE_GATE_CAST_BLOCKS = 128
MOE_OUT_CAST_BLOCKS = 256


def _routing_tables(route, counts, n_tiles):
    idx1 = route[0].astype(jnp.int32)
    idx2 = route[1].astype(jnp.int32)
    cnt = counts[:, 0].astype(jnp.int32)
    tiles_e = (cnt + MOE_TM - 1) // MOE_TM
    tile_end = jnp.cumsum(tiles_e)
    offset = (tile_end - tiles_e) * MOE_TM
    pos1 = offset[idx1] + route[4].astype(jnp.int32)
    pos2 = offset[idx2] + route[5].astype(jnp.int32)
    n_used = tile_end[-1]
    tiles = jnp.arange(n_tiles, dtype=jnp.int32)
    te = jnp.sum((tiles[:, None] >= tile_end[None, :]).astype(jnp.int32), axis=1)
    te = jnp.minimum(te, N_EXPERTS - 1)
    te = jnp.where(tiles < n_used, te, te[n_used - 1])
    last_of_expert = jnp.any((tiles[:, None] == tile_end[None, :] - 1) & (tiles_e[None, :] > 0),
                             axis=1)
    fill_flag = (last_of_expert | (tiles >= n_used)).astype(jnp.int32)
    live = cnt[te] - (tiles - (tile_end - tiles_e)[te]) * MOE_TM
    tile_rows = jnp.where(tiles < n_used, jnp.clip(live, 0, MOE_TM), 0).astype(jnp.int32)
    pos_flat = jnp.concatenate([pos1, pos2]).astype(jnp.int32)
    gates = jnp.stack([route[2], route[3]], axis=1)
    return pos_flat, gates, te, n_used.reshape(1).astype(jnp.int32), fill_flag, tile_rows


def kernel(x, ln_mix, ln_ffn, ln_final, ret_w_in, ret_w_out, ffn_w_in, ffn_w_out,
           att_w_in, att_w_out, att_rel_bias, moe_router, moe_w_in, moe_w_out):
    b, s, d = x.shape
    t = b * s
    x2 = x.reshape(t, d)
    f_dense = ffn_w_out.shape[1]
    f_expert = moe_w_out.shape[2]

    moe_in2 = moe_w_in[0].reshape(N_EXPERTS * d, 2 * f_expert)
    moe_out2 = moe_w_out[0].reshape(N_EXPERTS * f_expert, d)
    ret_steps = b * RET_HEADS
    att_steps = ATT_GROUPS * b
    cast_moe_gate = _SideCast(moe_in2, MOE_GATE_CAST_BLOCKS, f_expert, 0)
    cast_moe_up = _SideCast(moe_in2, att_steps, f_expert, 1)
    cast_moe_out = _SideCast(moe_out2, MOE_OUT_CAST_BLOCKS, d)
    cast_ffn_gate = _SideCast(ffn_w_in[0], ret_steps, f_dense, 0)
    cast_ffn_up = _SideCast(ffn_w_in[0], ret_steps, f_dense, 1)
    cast_ffn_out = _SideCast(ffn_w_out[0], ret_steps // 2, d)

    h = _rmsnorm(x2, ln_mix[0], BF16)
    proj, (moe_gate_bf,) = _matmul(h, ret_w_in[0], None, BF16, PROJ_TM, PROJ_TN, "ret_proj_in",
                                   sides=[cast_moe_gate])
    pos = jnp.arange(s, dtype=F32)
    inv_freq = ROPE_BASE ** (-jnp.arange(0, RET_QK_DIM, 2, dtype=F32) / RET_QK_DIM)
    ang = pos[:, None] * inv_freq[None, :]
    y, (ffn_gate_bf, ffn_up_bf, ffn_out_bf) = _retention(
        proj.reshape(b, s, -1), jnp.cos(ang), jnp.sin(ang),
        sides=[cast_ffn_gate, cast_ffn_up, cast_ffn_out])
    x2, _ = _matmul(y.reshape(t, RET_HV), ret_w_out[0], x2, F32, PROJ_TM, PROJ_TN // 2,
                    "ret_proj_out")

    dense_tiles = t // MOE_TM
    (x2, h), (moe_out_bf,) = _ffn(
        x2, (ln_ffn[0], ln_mix[1]), ffn_gate_bf[None], ffn_up_bf[None], ffn_out_bf[None],
        jnp.zeros((dense_tiles,), jnp.int32), jnp.full((1,), dense_tiles, jnp.int32),
        jnp.full((dense_tiles,), MOE_TM, jnp.int32),
        MOE_TM, DENSE_TF, 48, "dense_swiglu", sides=[cast_moe_out])

    qkv, _ = _matmul(h, att_w_in[0], None, BF16, PROJ_TM, PROJ_TN, "att_proj_in")
    a, (moe_up_bf,) = _attention(qkv.reshape(b, s, -1), att_rel_bias[0], sides=[cast_moe_up])
    x2, _ = _matmul(a.reshape(t, d), att_w_out[0], x2, F32, PROJ_TM, PROJ_TN, "att_proj_out")

    route, counts = _router(x2, ln_ffn[1], moe_router[0])
    n_tiles = 2 * t // MOE_TM + N_EXPERTS
    pos_flat, gates, te, n_used, fill_flag, tile_rows = _routing_tables(route, counts, n_tiles)
    xs = _dispatch(x2, ln_ffn[1], pos_flat, fill_flag, n_tiles)
    (ys,), _ = _ffn(xs, None, moe_gate_bf.reshape(N_EXPERTS, d, f_expert),
                 moe_up_bf.reshape(N_EXPERTS, d, f_expert),
                 moe_out_bf.reshape(N_EXPERTS, f_expert, d), te, n_used, tile_rows,
                 MOE_TM, EXPERT_TF, 56, "expert_swiglu", half_tiles=True)
    out = _combine(x2, ys, pos_flat, gates, ln_final)
    return out.reshape(b, s, d)
```
